```python
import math
import jax
import jax.numpy as jnp
from jax import lax
import numpy as np

D_MODEL = 2048
BATCH = 8
SEQ = 4096
DEPTH = 2

CTX_LEN = 256
GRID_W = 64
HEAD_DIM = 128
ROPE_QUARTER = HEAD_DIM // 4
ROPE_THETA = 10000.0
Q_BLOCK = 128
NORM_EPS = 1e-6
D_FF = 5632
N_MOD = 9

HG_HEADS = 8
HG_DK = 128
HG_DV = 128
HG_WIDTH = HG_HEADS * HG_DK
HG_CHUNK = 64
DA_HEADS = 4
DA_DK = 128
DA_DV = 2 * DA_DK
DA_QK_WIDTH = 2 * DA_HEADS * DA_DK
DA_V_WIDTH = DA_HEADS * DA_DV
EVEN_IN = 5 * HG_WIDTH + 2 * DA_QK_WIDTH + DA_V_WIDTH
EVEN_OUT = HG_HEADS * HG_DV + DA_V_WIDTH

GQA_HEADS = 8
GQA_KV_HEADS = 2
GQA_Q_WIDTH = GQA_HEADS * HEAD_DIM
GQA_KV_WIDTH = GQA_KV_HEADS * HEAD_DIM
RET_HEADS = 4
RET_DK = 128
RET_DV = 256
RET_CHUNK = 128
RET_QK_WIDTH = RET_HEADS * RET_DK
RET_V_WIDTH = RET_HEADS * RET_DV
ODD_IN = GQA_Q_WIDTH + 2 * GQA_KV_WIDTH + 2 * RET_QK_WIDTH + 2 * RET_V_WIDTH
ODD_OUT = GQA_Q_WIDTH + RET_V_WIDTH

N_EVEN = (DEPTH + 1) // 2
N_ODD = DEPTH // 2

kernel_name = "hybrid_hgrn2_diffattn_gqa_retention_dit"


def rms_norm(x, g):
    xf = x.astype(jnp.float32)
    y = xf * lax.rsqrt(jnp.mean(xf * xf, axis=-1, keepdims=True) + NORM_EPS)
    return (y * g.astype(jnp.float32)).astype(x.dtype)


def modulate(x, g, shift, scale):
    return rms_norm(x, g) * (1.0 + scale) + shift


def swiglu(h, w13, w2):
    a, b = jnp.split(h @ w13, 2, axis=-1)
    return (jax.nn.silu(a) * b) @ w2


def rope_tables(n_tokens):
    rows = n_tokens // GRID_W
    row = jnp.repeat(jnp.arange(rows), GRID_W)
    col = jnp.tile(jnp.arange(GRID_W), rows)
    pos = jnp.stack([row, col], axis=-1).astype(jnp.float32)
    inv = ROPE_THETA ** (-jnp.arange(ROPE_QUARTER, dtype=jnp.float32) / ROPE_QUARTER)
    ang = pos[:, :, None] * inv
    return jnp.cos(ang)[:, None], jnp.sin(ang)[:, None]


def apply_rope(x, cos, sin):
    xr = x.reshape(*x.shape[:-1], 2, 2, ROPE_QUARTER)
    x1, x2 = xr[..., 0, :], xr[..., 1, :]
    c, s = cos.astype(x.dtype), sin.astype(x.dtype)
    return jnp.stack([x1 * c - x2 * s, x2 * c + x1 * s], axis=-2).reshape(x.shape)


def block_attention(q, k, v):
    b, t, h, dk = q.shape
    nb = t // Q_BLOCK
    qb = jnp.moveaxis(q.reshape(b, nb, Q_BLOCK, h, dk), 1, 0)

    def attend(qi):
        s = jnp.einsum('bqhd,bkhd->bhqk', qi, k).astype(jnp.float32)
        p = jax.nn.softmax(s, axis=-1).astype(v.dtype)
        return jnp.einsum('bhqk,bkhd->bqhd', p, v)

    o = lax.map(attend, qb)
    return jnp.moveaxis(o, 0, 1).reshape(b, t, h, v.shape[-1])


def gla_chunked(q, k, v, log_f, s0):
    b, h, t, dk = q.shape
    n = t // HG_CHUNK

    def chunks(a):
        return jnp.moveaxis(a.reshape(b, h, n, HG_CHUNK, a.shape[-1]), 2, 0)

    lower = jnp.tril(jnp.ones((HG_CHUNK, HG_CHUNK), dtype=bool))[:, :, None]

    def step(state, inp):
        qc, kc, vc, gc = inp
        bcum = jnp.cumsum(gc, axis=2)
        rel = jnp.where(lower, bcum[:, :, :, None, :] - bcum[:, :, None, :, :], -jnp.inf)
        scores = jnp.einsum('bhik,bhjk,bhijk->bhij', qc, kc, jnp.exp(rel))
        out = scores @ vc + jnp.einsum('bhik,bhkv->bhiv', qc * jnp.exp(bcum), state)
        b_last = bcum[:, :, -1:, :]
        state = (jnp.swapaxes(jnp.exp(b_last), -1, -2) * state
                 + jnp.einsum('bhjk,bhjv->bhkv', kc * jnp.exp(b_last - bcum), vc))
        return state, out

    s_fin, o = lax.scan(step, s0, (chunks(q), chunks(k), chunks(v), chunks(log_f)))
    return jnp.moveaxis(o, 0, 2).reshape(b, h, t, v.shape[-1]), s_fin


def retention_chunked(q, k, v, log_gamma, s0):
    b, h, t, dk = q.shape
    n = t // RET_CHUNK
    pos = jnp.arange(RET_CHUNK, dtype=jnp.float32)
    rel = pos[:, None] - pos[None, :]
    lg = log_gamma[:, None, None]
    decay = jnp.where(rel >= 0, jnp.exp(jnp.maximum(rel, 0.0) * lg), 0.0)
    q_dec = jnp.exp((pos + 1.0)[None, :, None] * lg)
    k_dec = jnp.exp((RET_CHUNK - 1.0 - pos)[None, :, None] * lg)
    c_dec = jnp.exp(RET_CHUNK * lg)

    def chunks(a):
        return jnp.moveaxis(a.reshape(b, h, n, RET_CHUNK, a.shape[-1]), 2, 0)

    def step(state, inp):
        qc, kc, vc = inp
        scores = jnp.einsum('bhid,bhjd->bhij', qc, kc) * decay
        out = scores @ vc + jnp.einsum('bhid,bhdv->bhiv', qc * q_dec, state)
        state = c_dec * state + jnp.einsum('bhjd,bhjv->bhdv', kc * k_dec, vc)
        return state, out

    s_fin, o = lax.scan(step, s0, (chunks(q), chunks(k), chunks(v)))
    return jnp.moveaxis(o, 0, 2).reshape(b, h, t, v.shape[-1]), s_fin


def flip_t(a):
    return a[:, :, ::-1]


def two_stream(scan, ctx_args, lat_args, s0):
    o_c, s_c = scan(*ctx_args, s0)
    o_x, _ = scan(*lat_args, s_c)
    return o_x, o_c


def bidir_two_stream(scan_f, scan_b, ctx_f, lat_f, ctx_b, lat_b, s0):
    ox_f, oc_f = two_stream(scan_f, ctx_f, lat_f, s0)
    ox_b, oc_b = two_stream(scan_b, tuple(flip_t(a) for a in ctx_b), tuple(flip_t(a) for a in lat_b), s0)
    return ox_f + flip_t(ox_b), oc_f + flip_t(oc_b)


def to_heads(a, n_heads):
    b, t, _ = a.shape
    return a.reshape(b, t, n_heads, -1).transpose(0, 2, 1, 3).astype(jnp.float32)


def even_mixer(h_x, h_c, w_in, w_out, lower_bound, hg_g, lam_p, da_g, layer_idx, cos, sin, ctx_out):
    dt = h_x.dtype
    lam_init = 0.8 - 0.6 * math.exp(-0.3 * layer_idx)
    lam = jnp.exp(jnp.sum(lam_p[0] * lam_p[1])) - jnp.exp(jnp.sum(lam_p[2] * lam_p[3])) + lam_init
    splits = [int(s) for s in np.cumsum([HG_WIDTH] * 5 + [DA_QK_WIDTH, DA_QK_WIDTH])]
    px = jnp.split(h_x @ w_in, splits, axis=-1)
    pc = jnp.split(h_c @ w_in, splits, axis=-1)

    lb_h = lower_bound.reshape(HG_HEADS, 1, HG_DK)

    def hg_prep(p):
        q = jax.nn.silu(to_heads(p[0], HG_HEADS)) * HG_DK ** -0.5
        val = to_heads(p[3], HG_HEADS)

        def forget(z):
            z = to_heads(z, HG_HEADS)
            return (1.0 - lb_h) * jax.nn.sigmoid(-z), jnp.log(lb_h + (1.0 - lb_h) * jax.nn.sigmoid(z))

        k_f, lf_f = forget(p[1])
        k_b, lf_b = forget(p[2])
        return (q, k_f, val, lf_f), (q, k_b, val, lf_b)

    ctx_f, ctx_b = hg_prep(pc)
    lat_f, lat_b = hg_prep(px)
    s0 = jnp.zeros((h_x.shape[0], HG_HEADS, HG_DK, HG_DV), jnp.float32)
    o_hx, o_hc = bidir_two_stream(gla_chunked, gla_chunked, ctx_f, lat_f, ctx_b, lat_b, s0)

    def hg_out(o, gate):
        b, h, t, d = o.shape
        y = rms_norm(o.transpose(0, 2, 1, 3), hg_g) * jax.nn.sigmoid(gate.astype(jnp.float32)).reshape(b, t, h, d)
        return y.reshape(b, t, h * d).astype(dt)

    def da_prep(p, rotate):
        b, t, _ = p[5].shape
        q = p[5].reshape(b, t, 2 * DA_HEADS, DA_DK)
        k = p[6].reshape(b, t, 2 * DA_HEADS, DA_DK)
        v = p[7].reshape(b, t, DA_HEADS, DA_DV)
        if rotate:
            q, k = apply_rope(q, cos, sin), apply_rope(k, cos, sin)
        return q * DA_DK ** -0.5, k, jnp.repeat(v, 2, axis=2)

    def da_out(o):
        b, t = o.shape[:2]
        o5 = o.reshape(b, t, DA_HEADS, 2, DA_DV)
        d = o5[..., 0, :] - lam * o5[..., 1, :]
        return (rms_norm(d, da_g) * (1.0 - lam_init)).reshape(b, t, DA_V_WIDTH).astype(dt)

    q_x, k_x, v_x = da_prep(px, True)
    q_c, k_c, v_c = da_prep(pc, False)
    o_dx = block_attention(q_x, jnp.concatenate([k_x, k_c], axis=1), jnp.concatenate([v_x, v_c], axis=1))
    y_x = jnp.concatenate([hg_out(o_hx, px[4]), da_out(o_dx)], axis=-1) @ w_out
    if not ctx_out:
        return y_x, None
    o_dc = block_attention(q_c, k_c, v_c)
    y_c = jnp.concatenate([hg_out(o_hc, pc[4]), da_out(o_dc)], axis=-1) @ w_out
    return y_x, y_c


def odd_mixer(h_x, h_c, w_in, w_out, q_g, k_g, decay_logit, ret_g, cos, sin, ctx_out):
    dt = h_x.dtype
    splits = [int(s) for s in np.cumsum([GQA_Q_WIDTH, GQA_KV_WIDTH, GQA_KV_WIDTH,
                                         RET_QK_WIDTH, RET_QK_WIDTH, RET_V_WIDTH])]
    px = jnp.split(h_x @ w_in, splits, axis=-1)
    pc = jnp.split(h_c @ w_in, splits, axis=-1)

    rep = GQA_HEADS // GQA_KV_HEADS

    def gqa_prep(p, rotate):
        b, t, _ = p[0].shape
        q = rms_norm(p[0].reshape(b, t, GQA_HEADS, HEAD_DIM), q_g)
        k = rms_norm(p[1].reshape(b, t, GQA_KV_HEADS, HEAD_DIM), k_g)
        v = p[2].reshape(b, t, GQA_KV_HEADS, HEAD_DIM)
        if rotate:
            q, k = apply_rope(q, cos, sin), apply_rope(k, cos, sin)
        return q * HEAD_DIM ** -0.5, jnp.repeat(k, rep, axis=2), jnp.repeat(v, rep, axis=2)

    q_x, k_x, v_x = gqa_prep(px, True)
    q_c, k_c, v_c = gqa_prep(pc, False)
    b, t = h_x.shape[:2]
    o_gx = block_attention(q_x, jnp.concatenate([k_x, k_c], axis=1),
                           jnp.concatenate([v_x, v_c], axis=1)).reshape(b, t, GQA_Q_WIDTH)

    log_gamma = jax.nn.log_sigmoid(decay_logit.astype(jnp.float32))

    def ret_prep(p, rotate):
        bb, tt, _ = p[3].shape
        q = p[3].reshape(bb, tt, RET_HEADS, RET_DK)
        k = p[4].reshape(bb, tt, RET_HEADS, RET_DK)
        v = p[5].reshape(bb, tt, RET_HEADS, RET_DV)
        if rotate:
            q, k = apply_rope(q, cos, sin), apply_rope(k, cos, sin)
        k = k * RET_DK ** -0.5
        return tuple(a.transpose(0, 2, 1, 3).astype(jnp.float32) for a in (q, k, v))

    def scan_f(q, k, v, s):
        return retention_chunked(q, k, v, log_gamma[0], s)

    def scan_b(q, k, v, s):
        return retention_chunked(q, k, v, log_gamma[1], s)

    ctx_r = ret_prep(pc, False)
    lat_r = ret_prep(px, True)
    s0 = jnp.zeros((b, RET_HEADS, RET_DK, RET_DV), jnp.float32)
    o_rx, o_rc = bidir_two_stream(scan_f, scan_b, ctx_r, lat_r, ctx_r, lat_r, s0)

    def ret_out(o, gate):
        bb, h, tt, d = o.shape
        y = rms_norm(o.transpose(0, 2, 1, 3), ret_g) * jax.nn.silu(gate.astype(jnp.float32)).reshape(bb, tt, h, d)
        return y.reshape(bb, tt, h * d).astype(dt)

    y_x = jnp.concatenate([o_gx.astype(dt), ret_out(o_rx, px[6])], axis=-1) @ w_out
    if not ctx_out:
        return y_x, None
    o_gc = block_attention(q_c, k_c, v_c).reshape(h_c.shape[0], h_c.shape[1], GQA_Q_WIDTH)
    y_c = jnp.concatenate([o_gc.astype(dt), ret_out(o_rc, pc[6])], axis=-1) @ w_out
    return y_x, y_c


def setup_inputs(seed: int = 0) -> dict:
    key = jax.random.key(seed)
    ks = jax.random.split(key, 22)
    f32 = jnp.float32

    def nrm(i, shape, scale):
        return scale * jax.random.normal(ks[i], shape, f32)

    def gain(i, shape):
        return 1.0 + nrm(i, shape, 0.02)

    gamma0 = 1.0 - 2.0 ** (-5.0 - np.arange(RET_HEADS, dtype=np.float32))
    base_logit = jnp.asarray(np.log(gamma0 / (1.0 - gamma0)), f32)
    return {
        "x": nrm(0, (BATCH, SEQ, D_MODEL), 1.0),
        "c": nrm(1, (BATCH, D_MODEL), 1.0),
        "ctx": nrm(2, (BATCH, CTX_LEN, D_MODEL), 1.0),
        "c_ctx": nrm(3, (D_MODEL,), 1.0),
        "mod_w": nrm(4, (DEPTH, D_MODEL, N_MOD * D_MODEL), D_MODEL ** -0.5),
        "mod_b": nrm(5, (DEPTH, N_MOD * D_MODEL), 0.01),
        "norm_g": gain(6, (DEPTH, 3, D_MODEL)),
        "ffn_w13": nrm(7, (DEPTH, 2, D_MODEL, 2 * D_FF), D_MODEL ** -0.5),
        "ffn_w2": nrm(8, (DEPTH, 2, D_FF, D_MODEL), D_FF ** -0.5),
        "even_w_in": nrm(9, (N_EVEN, D_MODEL, EVEN_IN), D_MODEL ** -0.5),
        "even_w_out": nrm(10, (N_EVEN, EVEN_OUT, D_MODEL), EVEN_OUT ** -0.5),
        "hgrn_lb_logits": nrm(11, (DEPTH + 1, HG_WIDTH), 0.1),
        "hgrn_norm_g": gain(12, (N_EVEN, HG_DV)),
        "diff_lambda": nrm(13, (N_EVEN, 4, DA_DK), 0.1),
        "diff_norm_g": gain(14, (N_EVEN, DA_DV)),
        "odd_w_in": nrm(15, (N_ODD, D_MODEL, ODD_IN), D_MODEL ** -0.5),
        "odd_w_out": nrm(16, (N_ODD, ODD_OUT, D_MODEL), ODD_OUT ** -0.5),
        "gqa_q_norm_g": gain(17, (N_ODD, HEAD_DIM)),
        "gqa_k_norm_g": gain(18, (N_ODD, HEAD_DIM)),
        "ret_decay_logit": base_logit + nrm(19, (N_ODD, 2, RET_HEADS), 0.1),
        "ret_norm_g": gain(20, (N_ODD, RET_DV)),
        "final_norm_g": gain(21, (D_MODEL,)),
    }


def reference(x, c, ctx, c_ctx, mod_w, mod_b, norm_g, ffn_w13, ffn_w2, even_w_in, even_w_out,
              hgrn_lb_logits, hgrn_norm_g, diff_lambda, diff_norm_g, odd_w_in, odd_w_out,
              gqa_q_norm_g, gqa_k_norm_g, ret_decay_logit, ret_norm_g, final_norm_g):
    cos, sin = rope_tables(x.shape[1])
    lower_bounds = jnp.cumsum(jax.nn.softmax(hgrn_lb_logits.astype(jnp.float32), axis=0), axis=0)
    for l in range(DEPTH):
        last = l == DEPTH - 1
        m_x = (jax.nn.silu(c) @ mod_w[l] + mod_b[l])[:, None, :]
        m_c = jax.nn.silu(c_ctx) @ mod_w[l] + mod_b[l]
        sx = jnp.split(m_x, N_MOD, axis=-1)
        sc = jnp.split(m_c, N_MOD, axis=-1)

        x = x + 0.5 * sx[2] * swiglu(modulate(x, norm_g[l, 0], sx[0], sx[1]), ffn_w13[l, 0], ffn_w2[l, 0])
        ctx = ctx + 0.5 * sc[2] * swiglu(modulate(ctx, norm_g[l, 0], sc[0], sc[1]), ffn_w13[l, 0], ffn_w2[l, 0])

        h_x = modulate(x, norm_g[l, 1], sx[3], sx[4])
        h_c = modulate(ctx, norm_g[l, 1], sc[3], sc[4])
        if l % 2 == 0:
            e = l // 2
            o_x, o_c = even_mixer(h_x, h_c, even_w_in[e], even_w_out[e], lower_bounds[l], hgrn_norm_g[e],
                                  diff_lambda[e], diff_norm_g[e], l, cos, sin, not last)
        else:
            o = l // 2
            o_x, o_c = odd_mixer(h_x, h_c, odd_w_in[o], odd_w_out[o], gqa_q_norm_g[o], gqa_k_norm_g[o],
                                 ret_decay_logit[o], ret_norm_g[o], cos, sin, not last)
        x = x + sx[5] * o_x

        x = x + 0.5 * sx[8] * swiglu(modulate(x, norm_g[l, 2], sx[6], sx[7]), ffn_w13[l, 1], ffn_w2[l, 1])
        if not last:
            ctx = ctx + sc[5] * o_c
            ctx = ctx + 0.5 * sc[8] * swiglu(modulate(ctx, norm_g[l, 2], sc[6], sc[7]), ffn_w13[l, 1], ffn_w2[l, 1])
    return rms_norm(x, final_norm_g)
```

```python
import functools
import math

import numpy as np
import jax
import jax.numpy as jnp
from jax import lax
from jax.experimental import pallas as pl
from jax.experimental.pallas import tpu as pltpu

F32 = jnp.float32
BF16 = jnp.bfloat16

NORM_EPS = 1e-6
GRID_W = 64
HEAD_DIM = 128
ROPE_QUARTER = HEAD_DIM // 4
ROPE_THETA = 10000.0
N_MOD = 9

HG_HEADS = 8
HG_CHUNK = 64
HG_SUB = 16
DA_HEADS = 4
DA_DV = 256
GQA_HEADS = 8
GQA_KV_HEADS = 2
RET_HEADS = 4
RET_DV = 256
RET_CHUNK = 128

VMEM_LIMIT = 56 * 1024 * 1024


def _sds(shape, dtype):
    return jax.ShapeDtypeStruct(shape, dtype)


def _params(*sem):
    return pltpu.CompilerParams(dimension_semantics=sem, vmem_limit_bytes=VMEM_LIMIT)


def _sig(x):
    return 1.0 / (1.0 + jnp.exp(-x))


def _rms(x, g):
    return x * lax.rsqrt(jnp.mean(x * x, axis=-1, keepdims=True) + NORM_EPS) * g


def _dot(a, b):
    return jnp.dot(a, b, preferred_element_type=F32)


def _dot_nt(a, b):
    return lax.dot_general(a, b, (((1,), (1,)), ((), ())), preferred_element_type=F32)


def _dot_tn(a, b):
    return lax.dot_general(a, b, (((0,), (0,)), ((), ())), preferred_element_type=F32)


def _dot_exact(a, b):
    return jnp.dot(a, b, preferred_element_type=F32, precision=lax.Precision.HIGHEST)


def _rope(x, cos, sa, sb):
    return x * cos + pltpu.roll(x, 96, 1) * sa + pltpu.roll(x, 32, 1) * sb


def _mod_kernel(c_ref, w_ref, b_ref, o_ref):
    c = c_ref[...]
    a = (c * _sig(c)).astype(BF16)
    o_ref[0] = _dot(a, w_ref[0].astype(BF16)) + b_ref[0]


def _mod_call(cc, mod_w, mod_b):
    L, D, N = mod_w.shape
    R = cc.shape[0]
    tn = 1024
    return pl.pallas_call(
        _mod_kernel, out_shape=_sds((L, R, N), F32), grid=(L, N // tn),
        in_specs=[pl.BlockSpec((R, D), lambda l, j: (0, 0)),
                  pl.BlockSpec((1, D, tn), lambda l, j: (l, 0, j)),
                  pl.BlockSpec((1, 1, tn), lambda l, j: (l, 0, j))],
        out_specs=pl.BlockSpec((1, R, tn), lambda l, j: (l, 0, j)),
        compiler_params=_params("parallel", "parallel"), name="mod",
    )(cc, mod_w, mod_b.reshape(L, 1, N))


def _ffn_kernel(*refs, final):
    if final:
        x_ref, sh_ref, sc_ref, gt_ref, g_ref, w1_ref, w3_ref, w2_ref, fg_ref, o_ref, h_scr, acc_scr = refs
    else:
        x_ref, sh_ref, sc_ref, gt_ref, g_ref, w1_ref, w3_ref, w2_ref, o_ref, h_scr, acc_scr = refs
    j = pl.program_id(2)

    @pl.when(j == 0)
    def _():
        h = _rms(x_ref[0], g_ref[...]) * (1.0 + sc_ref[0]) + sh_ref[0]
        h_scr[...] = h.astype(BF16)

    h = h_scr[...]
    a = _dot(h, w1_ref[...])
    b = _dot(h, w3_ref[...])
    u = (a * _sig(a) * b).astype(BF16)
    y = _dot(u, w2_ref[...])

    @pl.when(j == 0)
    def _():
        acc_scr[...] = y

    @pl.when(j > 0)
    def _():
        acc_scr[...] += y

    @pl.when(j == pl.num_programs(2) - 1)
    def _():
        out = x_ref[0] + 0.5 * gt_ref[0] * acc_scr[...]
        if final:
            out = _rms(out, fg_ref[...])
        o_ref[0] = out


def _ffn_call(x, sh, sc, gt, g, w13, w2, l, i, final_g=None, tm=512, tf=512):
    B, T, D = x.shape
    F = w2.shape[2]
    nf = F // tf
    tm = min(tm, T)
    vec = pl.BlockSpec((1, 1, D), lambda b, t, j: (b, 0, 0))
    row = pl.BlockSpec((1, D), lambda b, t, j: (0, 0))
    in_specs = [pl.BlockSpec((1, tm, D), lambda b, t, j: (b, t, 0)), vec, vec, vec, row,
                pl.BlockSpec((None, None, D, tf), lambda b, t, j: (l, i, 0, j)),
                pl.BlockSpec((None, None, D, tf), lambda b, t, j: (l, i, 0, nf + j)),
                pl.BlockSpec((None, None, tf, D), lambda b, t, j: (l, i, j, 0))]
    args = [x, sh, sc, gt, g.reshape(1, D), w13, w13, w2]
    if final_g is not None:
        in_specs.append(row)
        args.append(final_g.reshape(1, D))
    return pl.pallas_call(
        functools.partial(_ffn_kernel, final=final_g is not None),
        out_shape=_sds((B, T, D), F32), grid=(B, T // tm, nf),
        in_specs=in_specs, out_specs=pl.BlockSpec((1, tm, D), lambda b, t, j: (b, t, 0)),
        scratch_shapes=[pltpu.VMEM((tm, D), BF16), pltpu.VMEM((tm, D), F32)],
        compiler_params=_params("parallel", "parallel", "arbitrary"), name="ffn",
    )(*args)


def _inproj_kernel(x_ref, sh_ref, sc_ref, g_ref, w_ref, o_ref, h_scr):
    @pl.when(pl.program_id(2) == 0)
    def _():
        h = _rms(x_ref[0], g_ref[...]) * (1.0 + sc_ref[0]) + sh_ref[0]
        h_scr[...] = h.astype(BF16)

    o_ref[0] = _dot(h_scr[...], w_ref[...])


def _inproj_call(x, sh, sc, g, w, tn, tm=512):
    B, T, D = x.shape
    N = w.shape[1]
    tm = min(tm, T)
    vec = pl.BlockSpec((1, 1, D), lambda b, t, j: (b, 0, 0))
    return pl.pallas_call(
        _inproj_kernel, out_shape=_sds((B, T, N), F32), grid=(B, T // tm, N // tn),
        in_specs=[pl.BlockSpec((1, tm, D), lambda b, t, j: (b, t, 0)), vec, vec,
                  pl.BlockSpec((1, D), lambda b, t, j: (0, 0)),
                  pl.BlockSpec((D, tn), lambda b, t, j: (0, j))],
        out_specs=pl.BlockSpec((1, tm, tn), lambda b, t, j: (b, t, j)),
        scratch_shapes=[pltpu.VMEM((tm, D), BF16)],
        compiler_params=_params("parallel", "parallel", "arbitrary"), name="inproj",
    )(x, sh, sc, g.reshape(1, D), w)


def _prep_even_kernel(*refs, rotate):
    if rotate:
        q_ref, k_ref, v_ref, cos_ref, sa_ref, sb_ref, qo_ref, ko_ref, vo_ref = refs
        cos, sa, sb = cos_ref[...], sa_ref[...], sb_ref[...]
    else:
        q_ref, k_ref, v_ref, qo_ref, ko_ref, vo_ref = refs
    scale = HEAD_DIM ** -0.5
    for h in range(2 * DA_HEADS):
        sl = slice(h * HEAD_DIM, (h + 1) * HEAD_DIM)
        q = q_ref[0, :, sl]
        k = k_ref[0, :, sl]
        if rotate:
            q = _rope(q, cos, sa, sb)
            k = _rope(k, cos, sa, sb)
        qo_ref[0, :, sl] = (q * scale).astype(BF16)
        ko_ref[0, :, sl] = k.astype(BF16)
    vo_ref[0] = v_ref[0].astype(BF16)


def _prep_even_call(p, tables, tp=512):
    B, T, _ = p.shape
    tp = min(tp, T)
    W = 2 * DA_HEADS * HEAD_DIM
    rotate = tables is not None
    in_specs = [pl.BlockSpec((1, tp, W), lambda b, t: (b, t, 5)),
                pl.BlockSpec((1, tp, W), lambda b, t: (b, t, 6)),
                pl.BlockSpec((1, tp, W), lambda b, t: (b, t, 7))]
    args = [p, p, p]
    if rotate:
        in_specs += [pl.BlockSpec((tp, HEAD_DIM), lambda b, t: (t, 0))] * 3
        args += list(tables)
    out = pl.BlockSpec((1, tp, W), lambda b, t: (b, t, 0))
    return pl.pallas_call(
        functools.partial(_prep_even_kernel, rotate=rotate),
        out_shape=[_sds((B, T, W), BF16)] * 3, grid=(B, T // tp),
        in_specs=in_specs, out_specs=[out, out, out],
        compiler_params=_params("parallel", "parallel"), name="prep_even",
    )(*args)


def _prep_odd_kernel(*refs, rotate):
    if rotate:
        q_ref, k_ref, v_ref, qg_ref, kg_ref, cos_ref, sa_ref, sb_ref, qo_ref, ko_ref, vo_ref = refs
        cos, sa, sb = cos_ref[...], sa_ref[...], sb_ref[...]
    else:
        q_ref, k_ref, v_ref, qg_ref, kg_ref, qo_ref, ko_ref, vo_ref = refs
    scale = HEAD_DIM ** -0.5
    for h in range(GQA_HEADS):
        sl = slice(h * HEAD_DIM, (h + 1) * HEAD_DIM)
        q = _rms(q_ref[0, :, sl], qg_ref[...])
        if rotate:
            q = _rope(q, cos, sa, sb)
        qo_ref[0, :, sl] = (q * scale).astype(BF16)
    for h in range(GQA_KV_HEADS):
        sl = slice(h * HEAD_DIM, (h + 1) * HEAD_DIM)
        k = _rms(k_ref[0, :, sl], kg_ref[...])
        if rotate:
            k = _rope(k, cos, sa, sb)
        ko_ref[0, :, sl] = k.astype(BF16)
    vo_ref[0] = v_ref[0].astype(BF16)


def _prep_odd_call(p, q_g, k_g, tables, tp=512):
    B, T, _ = p.shape
    tp = min(tp, T)
    QW = GQA_HEADS * HEAD_DIM
    KW = GQA_KV_HEADS * HEAD_DIM
    rotate = tables is not None
    gain = pl.BlockSpec((1, HEAD_DIM), lambda b, t: (0, 0))
    in_specs = [pl.BlockSpec((1, tp, QW), lambda b, t: (b, t, 0)),
                pl.BlockSpec((1, tp, KW), lambda b, t: (b, t, QW // KW)),
                pl.BlockSpec((1, tp, KW), lambda b, t: (b, t, QW // KW + 1)), gain, gain]
    args = [p, p, p, q_g.reshape(1, HEAD_DIM), k_g.reshape(1, HEAD_DIM)]
    if rotate:
        in_specs += [pl.BlockSpec((tp, HEAD_DIM), lambda b, t: (t, 0))] * 3
        args += list(tables)
    return pl.pallas_call(
        functools.partial(_prep_odd_kernel, rotate=rotate),
        out_shape=[_sds((B, T, QW), BF16), _sds((B, T, KW), BF16), _sds((B, T, KW), BF16)],
        grid=(B, T // tp), in_specs=in_specs,
        out_specs=[pl.BlockSpec((1, tp, QW), lambda b, t: (b, t, 0)),
                   pl.BlockSpec((1, tp, KW), lambda b, t: (b, t, 0)),
                   pl.BlockSpec((1, tp, KW), lambda b, t: (b, t, 0))],
        compiler_params=_params("parallel", "parallel"), name="prep_odd",
    )(*args)


def _attn_kernel(*refs, n_q, shared_k, has_lat, diff, lam_init):
    refs = list(refs)
    q_ref = refs.pop(0)
    if has_lat:
        kl_ref, vl_ref = refs.pop(0), refs.pop(0)
    kc_ref, vc_ref = refs.pop(0), refs.pop(0)
    if diff:
        lam_ref, g_ref = refs.pop(0), refs.pop(0)
    o_ref = refs.pop(0)
    outs = []
    for h in range(n_q):
        q = q_ref[0, :, h * HEAD_DIM:(h + 1) * HEAD_DIM]
        ks = slice(0, HEAD_DIM) if shared_k else slice(h * HEAD_DIM, (h + 1) * HEAD_DIM)
        s_c = _dot_nt(q, kc_ref[0, :, ks])
        m = jnp.max(s_c, axis=-1, keepdims=True)
        if has_lat:
            s_l = _dot_nt(q, kl_ref[0, :, ks])
            m = jnp.maximum(m, jnp.max(s_l, axis=-1, keepdims=True))
        p_c = jnp.exp(s_c - m)
        den = jnp.sum(p_c, axis=-1, keepdims=True)
        acc = _dot(p_c.astype(BF16), vc_ref[0])
        if has_lat:
            p_l = jnp.exp(s_l - m)
            den = den + jnp.sum(p_l, axis=-1, keepdims=True)
            acc = acc + _dot(p_l.astype(BF16), vl_ref[0])
        o = acc / den
        if diff:
            outs.append(o)
        else:
            o_ref[0, :, h * HEAD_DIM:(h + 1) * HEAD_DIM] = o
    if diff:
        d = outs[0] - lam_ref[...] * outs[1]
        o_ref[0] = _rms(d, g_ref[...]) * (1.0 - lam_init)


def _attn_call(q, k_lat, v_lat, k_ctx, v_ctx, *, n_groups, n_q, shared_k, dv, diff=False, lam=None, gain=None,
               lam_init=0.0, tq=256):
    B, Tq, _ = q.shape
    C = k_ctx.shape[1]
    tq = min(tq, Tq)
    has_lat = k_lat is not None
    qw = n_q * HEAD_DIM
    kw = HEAD_DIM if shared_k else qw
    in_specs = [pl.BlockSpec((1, tq, qw), lambda b, g, t: (b, t, g))]
    args = [q]
    if has_lat:
        T = k_lat.shape[1]
        in_specs += [pl.BlockSpec((1, T, kw), lambda b, g, t: (b, 0, g)),
                     pl.BlockSpec((1, T, dv), lambda b, g, t: (b, 0, g))]
        args += [k_lat, v_lat]
    in_specs += [pl.BlockSpec((1, C, kw), lambda b, g, t: (b, 0, g)),
                 pl.BlockSpec((1, C, dv), lambda b, g, t: (b, 0, g))]
    args += [k_ctx, v_ctx]
    if diff:
        in_specs += [pl.BlockSpec((1, dv), lambda b, g, t: (0, 0))] * 2
        args += [lam, gain]
    ow = dv if diff else qw
    return pl.pallas_call(
        functools.partial(_attn_kernel, n_q=n_q, shared_k=shared_k, has_lat=has_lat, diff=diff, lam_init=lam_init),
        out_shape=_sds((B, Tq, n_groups * ow), F32), grid=(B, n_groups, Tq // tq),
        in_specs=in_specs, out_specs=pl.BlockSpec((1, tq, ow), lambda b, g, t: (b, t, g)),
        compiler_params=_params("parallel", "parallel", "arbitrary"), name="attn",
    )(*args)


_GLA_ROWS_B, _GLA_ROWS_W, _GLA_ROWS_C, _GLA_ROWS_A, _GLA_ROWS_ONE = 0, 64, 128, 192, 384
_GLA_ROWS = 392


def _gla_sum_matrix(reverse):
    n, s = HG_CHUNK, HG_SUB
    i = np.arange(n)[:, None]
    m = np.arange(n)[None, :]
    if reverse:
        i, m = n - 1 - i, n - 1 - m
    mats = [m <= i, (m <= i) & (m // s == i // s), m > i]
    for blk in range(1, n // s):
        mats.append((m > i) & (m < s * blk))
    mats.append(np.ones((8, n), bool))
    return np.concatenate([np.broadcast_to(a, (a.shape[0], n)) for a in mats], axis=0).astype(np.float32)


def _gla_chunk(q_ref, z_ref, v_ref, m_ref, o_ref, st_scr, d, r, lb, reverse):
    n, s = HG_CHUNK, HG_SUB
    nsub = n // s
    rows = pl.ds(r, n)
    qraw = q_ref[0, rows, :]
    z = z_ref[0, rows, :]
    v = v_ref[0, rows, :].astype(BF16)
    q = qraw * _sig(qraw) * (HEAD_DIM ** -0.5)
    k = (1.0 - lb) * _sig(-z)
    g = jnp.log(lb + (1.0 - lb) * _sig(z))
    sums = _dot_exact(m_ref[...], g)
    b = sums[_GLA_ROWS_B:_GLA_ROWS_B + n]
    w = sums[_GLA_ROWS_W:_GLA_ROWS_W + n]
    c = sums[_GLA_ROWS_C:_GLA_ROWS_C + n]
    b_last = sums[_GLA_ROWS_ONE:_GLA_ROWS_ONE + 1]
    st = st_scr[d]
    out = _dot_nt((q * jnp.exp(b)).astype(BF16), st.astype(BF16))
    qw = (q * jnp.exp(w)).astype(BF16)
    col = lax.broadcasted_iota(jnp.int32, (s, n), 1)
    ii = lax.broadcasted_iota(jnp.int32, (s, s), 0)
    jj = lax.broadcasted_iota(jnp.int32, (s, s), 1)
    tri = (jj >= ii) if reverse else (jj <= ii)
    pieces = []
    for blk in range(nsub):
        sl = slice(blk * s, (blk + 1) * s)
        wb = w[sl]
        e = jnp.exp(jnp.minimum(wb[:, None, :] - wb[None, :, :], 0.0))
        sd = jnp.sum(q[sl][:, None, :] * k[sl][None, :, :] * e, axis=-1)
        sd = jnp.where(tri, sd, 0.0)
        o_blk = _dot(sd.astype(BF16), v[sl])
        a_idx = (nsub - 1 - blk) if reverse else blk
        if a_idx >= 1:
            a = sums[_GLA_ROWS_A + (a_idx - 1) * n:_GLA_ROWS_A + a_idx * n]
            ka = (k * jnp.exp(a)).astype(BF16)
            so = _dot_nt(qw[sl], ka)
            keep = (col >= (blk + 1) * s) if reverse else (col < blk * s)
            so = jnp.where(keep, so, 0.0)
            o_blk = o_blk + _dot(so.astype(BF16), v)
        pieces.append(o_blk)
    o_ref[0, rows, :] = out + jnp.concatenate(pieces, axis=0)
    kc = (k * jnp.exp(c)).astype(BF16)
    st_scr[d] = st * jnp.exp(b_last) + _dot_tn(v, kc)


def _gla_kernel(qf_ref, zf_ref, vf_ref, qb_ref, zb_ref, vb_ref, lb_ref, mf_ref, mb_ref, s0_ref,
                of_ref, ob_ref, sfin_ref, st_scr, *, tb):
    i = pl.program_id(2)

    @pl.when(i == 0)
    def _():
        st_scr[...] = s0_ref[...]

    lb = lb_ref[...]
    nchunk = tb // HG_CHUNK

    def body(c, carry):
        rf = pl.multiple_of(c * HG_CHUNK, HG_CHUNK)
        _gla_chunk(qf_ref, zf_ref, vf_ref, mf_ref, of_ref, st_scr, 0, rf, lb, False)
        rb = pl.multiple_of((nchunk - 1 - c) * HG_CHUNK, HG_CHUNK)
        _gla_chunk(qb_ref, zb_ref, vb_ref, mb_ref, ob_ref, st_scr, 1, rb, lb, True)
        return carry

    lax.fori_loop(0, nchunk, body, 0)

    @pl.when(i == pl.num_programs(2) - 1)
    def _():
        sfin_ref[...] = st_scr[...]


def _gla_call(p, lb, s0, tb=512):
    B, T, _ = p.shape
    tb = min(tb, T)
    nb = T // tb
    H = HG_HEADS
    blk = (1, tb, HEAD_DIM)
    fwd = lambda c0: pl.BlockSpec(blk, lambda b, h, i: (b, i, c0 + h))
    bwd = lambda c0: pl.BlockSpec(blk, lambda b, h, i: (b, nb - 1 - i, c0 + h))
    mat = pl.BlockSpec((_GLA_ROWS, HG_CHUNK), lambda b, h, i: (0, 0))
    st = pl.BlockSpec((2, None, None, HEAD_DIM, HEAD_DIM), lambda b, h, i: (0, b, h, 0, 0))
    return pl.pallas_call(
        functools.partial(_gla_kernel, tb=tb),
        out_shape=[_sds((B, T, H * HEAD_DIM), F32), _sds((B, T, H * HEAD_DIM), F32),
                   _sds((2, B, H, HEAD_DIM, HEAD_DIM), F32)],
        grid=(B, H, nb),
        in_specs=[fwd(0), fwd(H), fwd(3 * H), bwd(0), bwd(2 * H), bwd(3 * H),
                  pl.BlockSpec((1, HEAD_DIM), lambda b, h, i: (0, h)), mat, mat, st],
        out_specs=[pl.BlockSpec(blk, lambda b, h, i: (b, i, h)),
                   pl.BlockSpec(blk, lambda b, h, i: (b, nb - 1 - i, h)), st],
        scratch_shapes=[pltpu.VMEM((2, HEAD_DIM, HEAD_DIM), F32)],
        compiler_params=_params("parallel", "parallel", "arbitrary"), name="gla",
    )(p, p, p, p, p, p, lb.reshape(1, H * HEAD_DIM),
      jnp.asarray(_gla_sum_matrix(False)), jnp.asarray(_gla_sum_matrix(True)), s0)


def _ret_chunk(q_ref, k_ref, v_ref, tabs, o_ref, st_scr, d, r, lg, reverse):
    n = RET_CHUNK
    rows = pl.ds(r, n)
    q = q_ref[0, rows, :]
    k = k_ref[0, rows, :]
    if tabs is not None:
        cos, sa, sb = (t[rows, :] for t in tabs)
        q = _rope(q, cos, sa, sb)
        k = _rope(k, cos, sa, sb)
    k = k * (HEAD_DIM ** -0.5)
    v = v_ref[0, rows, :].astype(BF16)
    ii = lax.broadcasted_iota(jnp.int32, (n, n), 0)
    jj = lax.broadcasted_iota(jnp.int32, (n, n), 1)
    rel = (jj - ii) if reverse else (ii - jj)
    decay = jnp.where(rel >= 0, jnp.exp(jnp.maximum(rel, 0).astype(F32) * lg), 0.0)
    pos = ((n - 1 - ii) if reverse else ii).astype(F32)
    q_dec = jnp.exp((pos + 1.0) * lg)
    k_dec = jnp.exp((n - 1.0 - pos) * lg)
    c_dec = jnp.exp(n * lg)
    st = st_scr[d]
    scores = _dot_nt(q.astype(BF16), k.astype(BF16)) * decay
    out = _dot(scores.astype(BF16), v) + _dot((q * q_dec).astype(BF16), st.astype(BF16))
    o_ref[0, rows, :] = out
    upd = _dot_tn((k * k_dec).astype(BF16), v)
    st_scr[d] = jnp.concatenate([c_dec, c_dec], axis=1) * st + upd


def _ret_kernel(*refs, tb, rotate):
    refs = list(refs)
    qf_ref, kf_ref, vf_ref, qb_ref, kb_ref, vb_ref = refs[:6]
    refs = refs[6:]
    tabs_f = tabs_b = None
    if rotate:
        tabs_f, tabs_b, refs = refs[:3], refs[3:6], refs[6:]
    lg_ref, s0_ref, of_ref, ob_ref, sfin_ref, st_scr = refs
    i = pl.program_id(2)

    @pl.when(i == 0)
    def _():
        st_scr[...] = s0_ref[...]

    nchunk = tb // RET_CHUNK
    lg_f = lg_ref[0]
    lg_b = lg_ref[1]

    def body(c, carry):
        rf = pl.multiple_of(c * RET_CHUNK, RET_CHUNK)
        _ret_chunk(qf_ref, kf_ref, vf_ref, tabs_f, of_ref, st_scr, 0, rf, lg_f, False)
        rb = pl.multiple_of((nchunk - 1 - c) * RET_CHUNK, RET_CHUNK)
        _ret_chunk(qb_ref, kb_ref, vb_ref, tabs_b, ob_ref, st_scr, 1, rb, lg_b, True)
        return carry

    lax.fori_loop(0, nchunk, body, 0)

    @pl.when(i == pl.num_programs(2) - 1)
    def _():
        sfin_ref[...] = st_scr[...]


def _ret_call(p, lg, s0, tables, tb=512):
    B, T, _ = p.shape
    tb = min(tb, T)
    nb = T // tb
    H = RET_HEADS
    rotate = tables is not None
    q0 = (GQA_HEADS + 2 * GQA_KV_HEADS)
    k0 = q0 + H
    v0 = (k0 + H) * HEAD_DIM // RET_DV
    qk = (1, tb, HEAD_DIM)
    vv = (1, tb, RET_DV)
    fwd = lambda shape, c0: pl.BlockSpec(shape, lambda b, h, i: (b, i, c0 + h))
    bwd = lambda shape, c0: pl.BlockSpec(shape, lambda b, h, i: (b, nb - 1 - i, c0 + h))
    in_specs = [fwd(qk, q0), fwd(qk, k0), fwd(vv, v0), bwd(qk, q0), bwd(qk, k0), bwd(vv, v0)]
    args = [p] * 6
    if rotate:
        in_specs += [pl.BlockSpec((tb, HEAD_DIM), lambda b, h, i: (i, 0))] * 3
        in_specs += [pl.BlockSpec((tb, HEAD_DIM), lambda b, h, i: (nb - 1 - i, 0))] * 3
        args += list(tables) * 2
    st = pl.BlockSpec((2, None, None, HEAD_DIM, RET_DV), lambda b, h, i: (0, b, h, 0, 0))
    in_specs += [pl.BlockSpec((2, None, 1, HEAD_DIM), lambda b, h, i: (0, h, 0, 0)), st]
    args += [lg, s0]
    return pl.pallas_call(
        functools.partial(_ret_kernel, tb=tb, rotate=rotate),
        out_shape=[_sds((B, T, H * RET_DV), F32), _sds((B, T, H * RET_DV), F32),
                   _sds((2, B, H, HEAD_DIM, RET_DV), F32)],
        grid=(B, H, nb), in_specs=in_specs,
        out_specs=[pl.BlockSpec(vv, lambda b, h, i: (b, i, h)),
                   pl.BlockSpec(vv, lambda b, h, i: (b, nb - 1 - i, h)), st],
        scratch_shapes=[pltpu.VMEM((2, HEAD_DIM, RET_DV), F32)],
        compiler_params=_params("parallel", "parallel", "arbitrary"), name="ret",
    )(*args)


def _outproj_even_kernel(of_ref, ob_ref, gate_ref, da_ref, x_ref, g5_ref, w_ref, hg_ref, o_ref, a_scr):
    W = HG_HEADS * HEAD_DIM
    for h in range(HG_HEADS):
        sl = slice(h * HEAD_DIM, (h + 1) * HEAD_DIM)
        o = of_ref[0, :, sl] + ob_ref[0, :, sl]
        a_scr[:, sl] = (_rms(o, hg_ref[...]) * _sig(gate_ref[0, :, sl])).astype(BF16)
    y = _dot(a_scr[...], w_ref[:W, :]) + _dot(da_ref[0].astype(BF16), w_ref[W:, :])
    o_ref[0] = x_ref[0] + g5_ref[0] * y


def _outproj_even_call(o_f, o_b, p, da, x, g5, w, hg_g, tm=256):
    B, T, D = x.shape
    tm = min(tm, T)
    W = HG_HEADS * HEAD_DIM
    half = pl.BlockSpec((1, tm, W), lambda b, t: (b, t, 0))
    full = pl.BlockSpec((1, tm, D), lambda b, t: (b, t, 0))
    return pl.pallas_call(
        _outproj_even_kernel, out_shape=_sds((B, T, D), F32), grid=(B, T // tm),
        in_specs=[half, half, pl.BlockSpec((1, tm, W), lambda b, t: (b, t, 4)), half, full,
                  pl.BlockSpec((1, 1, D), lambda b, t: (b, 0, 0)),
                  pl.BlockSpec(w.shape, lambda b, t: (0, 0)),
                  pl.BlockSpec((1, HEAD_DIM), lambda b, t: (0, 0))],
        out_specs=full, scratch_shapes=[pltpu.VMEM((tm, W), BF16)],
        compiler_params=_params("parallel", "parallel"), name="outproj_even",
    )(o_f, o_b, p, da, x, g5, w, hg_g.reshape(1, HEAD_DIM))


def _outproj_odd_kernel(og_ref, of_ref, ob_ref, gate_lo_ref, gate_hi_ref, x_ref, g5_ref, w_ref, rg_ref, o_ref, a_scr):
    W = GQA_HEADS * HEAD_DIM
    half_heads = RET_HEADS // 2
    for h in range(RET_HEADS):
        sl = slice(h * RET_DV, (h + 1) * RET_DV)
        o = of_ref[0, :, sl] + ob_ref[0, :, sl]
        gate_ref = gate_lo_ref if h < half_heads else gate_hi_ref
        gate = gate_ref[0, :, (h % half_heads) * RET_DV:(h % half_heads + 1) * RET_DV]
        a_scr[:, sl] = (_rms(o, rg_ref[...]) * (gate * _sig(gate))).astype(BF16)
    y = _dot(og_ref[0].astype(BF16), w_ref[:W, :]) + _dot(a_scr[...], w_ref[W:, :])
    o_ref[0] = x_ref[0] + g5_ref[0] * y


def _outproj_odd_call(o_g, o_f, o_b, p, x, g5, w, ret_g, tm=256):
    B, T, D = x.shape
    tm = min(tm, T)
    W = RET_HEADS * RET_DV
    gw = W // 2
    gate_blk, rem = divmod(p.shape[2] - W, gw)
    assert rem == 0
    half = pl.BlockSpec((1, tm, W), lambda b, t: (b, t, 0))
    full = pl.BlockSpec((1, tm, D), lambda b, t: (b, t, 0))
    return pl.pallas_call(
        _outproj_odd_kernel, out_shape=_sds((B, T, D), F32), grid=(B, T // tm),
        in_specs=[half, half, half, pl.BlockSpec((1, tm, gw), lambda b, t: (b, t, gate_blk)),
                  pl.BlockSpec((1, tm, gw), lambda b, t: (b, t, gate_blk + 1)), full,
                  pl.BlockSpec((1, 1, D), lambda b, t: (b, 0, 0)),
                  pl.BlockSpec(w.shape, lambda b, t: (0, 0)),
                  pl.BlockSpec((1, RET_DV), lambda b, t: (0, 0))],
        out_specs=full, scratch_shapes=[pltpu.VMEM((tm, W), BF16)],
        compiler_params=_params("parallel", "parallel"), name="outproj_odd",
    )(o_g, o_f, o_b, p, p, x, g5, w, ret_g.reshape(1, RET_DV))


def _rope_tables(n_tokens):
    rows = n_tokens // GRID_W
    row = jnp.repeat(jnp.arange(rows), GRID_W)
    col = jnp.tile(jnp.arange(GRID_W), rows)
    pos = jnp.stack([row, col], axis=-1).astype(F32)
    inv = ROPE_THETA ** (-jnp.arange(ROPE_QUARTER, dtype=F32) / ROPE_QUARTER)
    ang = pos[:, :, None] * inv
    cos, sin = jnp.cos(ang), jnp.sin(ang)
    zero = jnp.zeros_like(sin)
    lanes = lambda a, b: jnp.stack([a, b], axis=2).reshape(n_tokens, HEAD_DIM)
    return lanes(cos, cos), lanes(-sin, zero), lanes(zero, sin)


def _even_mixer(x, ctx, mx, mc, g, w_in, w_out, lb, hg_g, lam_p, da_g, layer_idx, tables, ctx_out):
    B = x.shape[0]
    lam_init = 0.8 - 0.6 * math.exp(-0.3 * layer_idx)
    lam = jnp.exp(jnp.sum(lam_p[0] * lam_p[1])) - jnp.exp(jnp.sum(lam_p[2] * lam_p[3])) + lam_init
    lam_row = jnp.full((1, DA_DV), lam, F32)
    da_gain = da_g.reshape(1, DA_DV)
    px = _inproj_call(x, mx[3], mx[4], g, w_in, tn=2048)
    pc = _inproj_call(ctx, mc[3], mc[4], g, w_in, tn=2048)

    s0 = jnp.zeros((2, B, HG_HEADS, HEAD_DIM, HEAD_DIM), F32)
    ocf, ocb, s_c = _gla_call(pc, lb, s0)
    oxf, oxb, _ = _gla_call(px, lb, s_c)

    qx, kx, vx = _prep_even_call(px, tables)
    qc, kc, vc = _prep_even_call(pc, None)
    attn = functools.partial(_attn_call, n_groups=DA_HEADS, n_q=2, shared_k=False, dv=DA_DV, diff=True,
                             lam=lam_row, gain=da_gain, lam_init=lam_init)
    da_x = attn(qx, kx, vx, kc, vc)
    x_new = _outproj_even_call(oxf, oxb, px, da_x, x, mx[5], w_out, hg_g)
    if not ctx_out:
        return x_new, None
    da_c = attn(qc, None, None, kc, vc)
    ctx_new = _outproj_even_call(ocf, ocb, pc, da_c, ctx, mc[5], w_out, hg_g)
    return x_new, ctx_new


def _odd_mixer(x, ctx, mx, mc, g, w_in, w_out, q_g, k_g, decay_logit, ret_g, tables, ctx_out):
    B = x.shape[0]
    px = _inproj_call(x, mx[3], mx[4], g, w_in, tn=w_in.shape[1] // 2)
    pc = _inproj_call(ctx, mc[3], mc[4], g, w_in, tn=w_in.shape[1] // 2)

    qx, kx, vx = _prep_odd_call(px, q_g, k_g, tables)
    qc, kc, vc = _prep_odd_call(pc, q_g, k_g, None)
    attn = functools.partial(_attn_call, n_groups=GQA_KV_HEADS, n_q=GQA_HEADS // GQA_KV_HEADS, shared_k=True,
                             dv=HEAD_DIM)
    og_x = attn(qx, kx, vx, kc, vc)

    log_gamma = jax.nn.log_sigmoid(decay_logit.astype(F32))
    lg = jnp.broadcast_to(log_gamma[:, :, None, None], (2, RET_HEADS, 1, HEAD_DIM))
    s0 = jnp.zeros((2, B, RET_HEADS, HEAD_DIM, RET_DV), F32)
    ocf, ocb, s_c = _ret_call(pc, lg, s0, None)
    oxf, oxb, _ = _ret_call(px, lg, s_c, tables)
    x_new = _outproj_odd_call(og_x, oxf, oxb, px, x, mx[5], w_out, ret_g)
    if not ctx_out:
        return x_new, None
    og_c = attn(qc, None, None, kc, vc)
    ctx_new = _outproj_odd_call(og_c, ocf, ocb, pc, ctx, mc[5], w_out, ret_g)
    return x_new, ctx_new


def kernel(x, c, ctx, c_ctx, mod_w, mod_b, norm_g, ffn_w13, ffn_w2, even_w_in, even_w_out, hgrn_lb_logits, hgrn_norm_g, diff_lambda, diff_norm_g, odd_w_in, odd_w_out, gqa_q_norm_g, gqa_k_norm_g, ret_decay_logit, ret_norm_g, final_norm_g):
    B, T, D = x.shape
    C = ctx.shape[1]
    depth = mod_w.shape[0]
    tables = _rope_tables(T)
    lower_bounds = jnp.cumsum(jax.nn.softmax(hgrn_lb_logits.astype(F32), axis=0), axis=0)

    rows = -(-(B + 1) // 8) * 8
    cc = jnp.zeros((rows, D), F32).at[:B].set(c).at[B].set(c_ctx)
    mod = _mod_call(cc, mod_w, mod_b)
    mod_x = mod[:, :B].reshape(depth, B, N_MOD, D).transpose(0, 2, 1, 3)[:, :, :, None, :]
    mod_c = mod[:, B].reshape(depth, N_MOD, 1, 1, D)

    w13 = ffn_w13.astype(BF16)
    w2 = ffn_w2.astype(BF16)
    even_in, even_out = even_w_in.astype(BF16), even_w_out.astype(BF16)
    odd_in, odd_out = odd_w_in.astype(BF16), odd_w_out.astype(BF16)

    ctx = ctx.reshape(1, B * C, D)
    for l in range(depth):
        last = l == depth - 1
        mx, mc = mod_x[l], mod_c[l]
        g = norm_g[l]
        x = _ffn_call(x, mx[0], mx[1], mx[2], g[0], w13, w2, l, 0)
        ctx = _ffn_call(ctx, mc[0], mc[1], mc[2], g[0], w13, w2, l, 0)
        ctx_b = ctx.reshape(B, C, D)
        mc_b = jnp.broadcast_to(mc, (N_MOD, B, 1, D))
        if l % 2 == 0:
            e = l // 2
            x, ctx_b = _even_mixer(x, ctx_b, mx, mc_b, g[1], even_in[e], even_out[e], lower_bounds[l], hgrn_norm_g[e],
                                   diff_lambda[e], diff_norm_g[e], l, tables, not last)
        else:
            o = l // 2
            x, ctx_b = _odd_mixer(x, ctx_b, mx, mc_b, g[1], odd_in[o], odd_out[o], gqa_q_norm_g[o], gqa_k_norm_g[o],
                                  ret_decay_logit[o], ret_norm_g[o], tables, not last)
        x = _ffn_call(x, mx[6], mx[7], mx[8], g[2], w13, w2, l, 1, final_g=final_norm_g if last else None)
        if not last:
            ctx = _ffn_call(ctx_b.reshape(1, B * C, D), mc[6], mc[7], mc[8], g[2], w13, w2, l, 1)
    return x
```

```python
import functools
import math

import numpy as np
import jax
import jax.numpy as jnp
from jax import lax
from jax.experimental import pallas as pl
from jax.experimental.pallas import tpu as pltpu

F32 = jnp.float32
BF16 = jnp.bfloat16

NORM_EPS = 1e-6
GRID_W = 64
HEAD_DIM = 128
ROPE_QUARTER = HEAD_DIM // 4
ROPE_THETA = 10000.0
N_MOD = 9

HG_HEADS = 8
HG_CHUNK = 64
HG_SUB = 8
DA_HEADS = 4
DA_DV = 256
GQA_HEADS = 8
GQA_KV_HEADS = 2
RET_HEADS = 4
RET_DV = 256
RET_CHUNK = 128

VMEM_LIMIT = 56 * 1024 * 1024


def _sds(shape, dtype):
    return jax.ShapeDtypeStruct(shape, dtype)


def _params(*sem):
    return pltpu.CompilerParams(dimension_semantics=sem, vmem_limit_bytes=VMEM_LIMIT)


def _sig(x):
    return 1.0 / (1.0 + jnp.exp(-x))


def _rms(x, g):
    return x * lax.rsqrt(jnp.mean(x * x, axis=-1, keepdims=True) + NORM_EPS) * g


def _dot(a, b):
    return jnp.dot(a, b, preferred_element_type=F32)


def _dot_nt(a, b):
    return lax.dot_general(a, b, (((1,), (1,)), ((), ())), preferred_element_type=F32)


def _dot_tn(a, b):
    return lax.dot_general(a, b, (((0,), (0,)), ((), ())), preferred_element_type=F32)


def _rope(x, cos, sa, sb):
    return x * cos + pltpu.roll(x, 96, 1) * sa + pltpu.roll(x, 32, 1) * sb


def _mod_kernel(c_ref, w_ref, b_ref, o_ref):
    c = c_ref[...]
    a = (c * _sig(c)).astype(BF16)
    o_ref[0] = _dot(a, w_ref[0].astype(BF16)) + b_ref[0]


def _mod_call(cc, mod_w, mod_b):
    L, D, N = mod_w.shape
    R = cc.shape[0]
    tn = 1024
    return pl.pallas_call(
        _mod_kernel, out_shape=_sds((L, R, N), F32), grid=(L, N // tn),
        in_specs=[pl.BlockSpec((R, D), lambda l, j: (0, 0)),
                  pl.BlockSpec((1, D, tn), lambda l, j: (l, 0, j)),
                  pl.BlockSpec((1, 1, tn), lambda l, j: (l, 0, j))],
        out_specs=pl.BlockSpec((1, R, tn), lambda l, j: (l, 0, j)),
        compiler_params=_params("parallel", "parallel"), name="mod",
    )(cc, mod_w, mod_b.reshape(L, 1, N))


def _ffn_kernel(*refs, final):
    if final:
        x_ref, sh_ref, sc_ref, gt_ref, g_ref, w1_ref, w3_ref, w2_ref, fg_ref, o_ref, h_scr, acc_scr = refs
    else:
        x_ref, sh_ref, sc_ref, gt_ref, g_ref, w1_ref, w3_ref, w2_ref, o_ref, h_scr, acc_scr = refs
    j = pl.program_id(2)

    @pl.when(j == 0)
    def _():
        h = _rms(x_ref[0], g_ref[...]) * (1.0 + sc_ref[0]) + sh_ref[0]
        h_scr[...] = h.astype(BF16)
        acc_scr[...] = jnp.zeros_like(acc_scr)

    h = h_scr[...]
    a = _dot(h, w1_ref[...])
    b = _dot(h, w3_ref[...])
    u = (a * _sig(a) * b).astype(BF16)
    acc_scr[...] += _dot(u, w2_ref[...])

    @pl.when(j == pl.num_programs(2) - 1)
    def _():
        out = x_ref[0] + 0.5 * gt_ref[0] * acc_scr[...]
        if final:
            out = _rms(out, fg_ref[...])
        o_ref[0] = out


def _ffn_call(x, sh, sc, gt, g, w13, w2, l, i, final_g=None, tm=512, tf=512):
    B, T, D = x.shape
    F = w2.shape[2]
    nf = F // tf
    tm = min(tm, T)
    vec = pl.BlockSpec((1, 1, D), lambda b, t, j: (b, 0, 0))
    row = pl.BlockSpec((1, D), lambda b, t, j: (0, 0))
    in_specs = [pl.BlockSpec((1, tm, D), lambda b, t, j: (b, t, 0)), vec, vec, vec, row,
                pl.BlockSpec((None, None, D, tf), lambda b, t, j: (l, i, 0, j)),
                pl.BlockSpec((None, None, D, tf), lambda b, t, j: (l, i, 0, nf + j)),
                pl.BlockSpec((None, None, tf, D), lambda b, t, j: (l, i, j, 0))]
    args = [x, sh, sc, gt, g.reshape(1, D), w13, w13, w2]
    if final_g is not None:
        in_specs.append(row)
        args.append(final_g.reshape(1, D))
    return pl.pallas_call(
        functools.partial(_ffn_kernel, final=final_g is not None),
        out_shape=_sds((B, T, D), F32), grid=(B, T // tm, nf),
        in_specs=in_specs, out_specs=pl.BlockSpec((1, tm, D), lambda b, t, j: (b, t, 0)),
        scratch_shapes=[pltpu.VMEM((tm, D), BF16), pltpu.VMEM((tm, D), F32)],
        compiler_params=_params("parallel", "parallel", "arbitrary"), name="ffn",
    )(*args)


def _inproj_kernel(x_ref, sh_ref, sc_ref, g_ref, w_ref, o_ref, h_scr):
    @pl.when(pl.program_id(2) == 0)
    def _():
        h = _rms(x_ref[0], g_ref[...]) * (1.0 + sc_ref[0]) + sh_ref[0]
        h_scr[...] = h.astype(BF16)

    o_ref[0] = _dot(h_scr[...], w_ref[...])


def _inproj_call(x, sh, sc, g, w, tn, tm=512):
    B, T, D = x.shape
    N = w.shape[1]
    tm = min(tm, T)
    vec = pl.BlockSpec((1, 1, D), lambda b, t, j: (b, 0, 0))
    return pl.pallas_call(
        _inproj_kernel, out_shape=_sds((B, T, N), F32), grid=(B, T // tm, N // tn),
        in_specs=[pl.BlockSpec((1, tm, D), lambda b, t, j: (b, t, 0)), vec, vec,
                  pl.BlockSpec((1, D), lambda b, t, j: (0, 0)),
                  pl.BlockSpec((D, tn), lambda b, t, j: (0, j))],
        out_specs=pl.BlockSpec((1, tm, tn), lambda b, t, j: (b, t, j)),
        scratch_shapes=[pltpu.VMEM((tm, D), BF16)],
        compiler_params=_params("parallel", "parallel", "arbitrary"), name="inproj",
    )(x, sh, sc, g.reshape(1, D), w)


def _prep_even_kernel(*refs, rotate):
    if rotate:
        q_ref, k_ref, v_ref, cos_ref, sa_ref, sb_ref, qo_ref, ko_ref, vo_ref = refs
        cos, sa, sb = cos_ref[...], sa_ref[...], sb_ref[...]
    else:
        q_ref, k_ref, v_ref, qo_ref, ko_ref, vo_ref = refs
    scale = HEAD_DIM ** -0.5
    for h in range(2 * DA_HEADS):
        sl = slice(h * HEAD_DIM, (h + 1) * HEAD_DIM)
        q = q_ref[0, :, sl]
        k = k_ref[0, :, sl]
        if rotate:
            q = _rope(q, cos, sa, sb)
            k = _rope(k, cos, sa, sb)
        qo_ref[0, :, sl] = (q * scale).astype(BF16)
        ko_ref[0, :, sl] = k.astype(BF16)
    vo_ref[0] = v_ref[0].astype(BF16)


def _prep_even_call(p, tables, tp=512):
    B, T, _ = p.shape
    tp = min(tp, T)
    W = 2 * DA_HEADS * HEAD_DIM
    rotate = tables is not None
    in_specs = [pl.BlockSpec((1, tp, W), lambda b, t: (b, t, 5)),
                pl.BlockSpec((1, tp, W), lambda b, t: (b, t, 6)),
                pl.BlockSpec((1, tp, W), lambda b, t: (b, t, 7))]
    args = [p, p, p]
    if rotate:
        in_specs += [pl.BlockSpec((tp, HEAD_DIM), lambda b, t: (t, 0))] * 3
        args += list(tables)
    out = pl.BlockSpec((1, tp, W), lambda b, t: (b, t, 0))
    return pl.pallas_call(
        functools.partial(_prep_even_kernel, rotate=rotate),
        out_shape=[_sds((B, T, W), BF16)] * 3, grid=(B, T // tp),
        in_specs=in_specs, out_specs=[out, out, out],
        compiler_params=_params("parallel", "parallel"), name="prep_even",
    )(*args)


def _prep_odd_kernel(*refs, rotate):
    if rotate:
        q_ref, k_ref, v_ref, qg_ref, kg_ref, cos_ref, sa_ref, sb_ref, qo_ref, ko_ref, vo_ref = refs
        cos, sa, sb = cos_ref[...], sa_ref[...], sb_ref[...]
    else:
        q_ref, k_ref, v_ref, qg_ref, kg_ref, qo_ref, ko_ref, vo_ref = refs
    scale = HEAD_DIM ** -0.5
    for h in range(GQA_HEADS):
        sl = slice(h * HEAD_DIM, (h + 1) * HEAD_DIM)
        q = _rms(q_ref[0, :, sl], qg_ref[...])
        if rotate:
            q = _rope(q, cos, sa, sb)
        qo_ref[0, :, sl] = (q * scale).astype(BF16)
    for h in range(GQA_KV_HEADS):
        sl = slice(h * HEAD_DIM, (h + 1) * HEAD_DIM)
        k = _rms(k_ref[0, :, sl], kg_ref[...])
        if rotate:
            k = _rope(k, cos, sa, sb)
        ko_ref[0, :, sl] = k.astype(BF16)
    vo_ref[0] = v_ref[0].astype(BF16)


def _prep_odd_call(p, q_g, k_g, tables, tp=512):
    B, T, _ = p.shape
    tp = min(tp, T)
    QW = GQA_HEADS * HEAD_DIM
    KW = GQA_KV_HEADS * HEAD_DIM
    rotate = tables is not None
    gain = pl.BlockSpec((1, HEAD_DIM), lambda b, t: (0, 0))
    in_specs = [pl.BlockSpec((1, tp, QW), lambda b, t: (b, t, 0)),
                pl.BlockSpec((1, tp, KW), lambda b, t: (b, t, QW // KW)),
                pl.BlockSpec((1, tp, KW), lambda b, t: (b, t, QW // KW + 1)), gain, gain]
    args = [p, p, p, q_g.reshape(1, HEAD_DIM), k_g.reshape(1, HEAD_DIM)]
    if rotate:
        in_specs += [pl.BlockSpec((tp, HEAD_DIM), lambda b, t: (t, 0))] * 3
        args += list(tables)
    return pl.pallas_call(
        functools.partial(_prep_odd_kernel, rotate=rotate),
        out_shape=[_sds((B, T, QW), BF16), _sds((B, T, KW), BF16), _sds((B, T, KW), BF16)],
        grid=(B, T // tp), in_specs=in_specs,
        out_specs=[pl.BlockSpec((1, tp, QW), lambda b, t: (b, t, 0)),
                   pl.BlockSpec((1, tp, KW), lambda b, t: (b, t, 0)),
                   pl.BlockSpec((1, tp, KW), lambda b, t: (b, t, 0))],
        compiler_params=_params("parallel", "parallel"), name="prep_odd",
    )(*args)


def _attn_kernel(*refs, n_q, shared_k, has_lat, diff, lam_init, kv_chunk):
    refs = list(refs)
    q_ref = refs.pop(0)
    if has_lat:
        kl_ref, vl_ref = refs.pop(0), refs.pop(0)
    kc_ref, vc_ref = refs.pop(0), refs.pop(0)
    if diff:
        lam_ref, g_ref = refs.pop(0), refs.pop(0)
    o_ref = refs.pop(0)
    tq = q_ref.shape[1]
    heads = [q_ref[0, :, h * HEAD_DIM:(h + 1) * HEAD_DIM] for h in range(n_q)]
    if shared_k:
        heads = [jnp.concatenate(heads, axis=0)]
    pieces = [(kc_ref, vc_ref, 0, kc_ref.shape[1])]
    if has_lat:
        pieces += [(kl_ref, vl_ref, c * kv_chunk, kv_chunk) for c in range(kl_ref.shape[1] // kv_chunk)]
    m = den = acc = None
    for k_ref, v_ref, start, size in pieces:
        rows = slice(start, start + size)
        s = jnp.concatenate([_dot_nt(qh, k_ref[0, rows, h * HEAD_DIM:(h + 1) * HEAD_DIM])
                             for h, qh in enumerate(heads)], axis=0)
        m_new = jnp.max(s, axis=-1, keepdims=True)
        if m is not None:
            m_new = jnp.maximum(m, m_new)
        p = jnp.exp(s - m_new)
        part = p[:, :HEAD_DIM]
        for c in range(1, size // HEAD_DIM):
            part = part + p[:, c * HEAD_DIM:(c + 1) * HEAD_DIM]
        pv = _dot(p.astype(BF16), v_ref[0, rows, :])
        if m is None:
            den, acc = part, pv
        else:
            a = jnp.exp(m - m_new)
            den = a * den + part
            acc = a * acc + pv
        m = m_new
    o = acc / jnp.sum(den, axis=-1, keepdims=True)
    if diff:
        d = o[:tq] - lam_ref[...] * o[tq:]
        o_ref[0] = _rms(d, g_ref[...]) * (1.0 - lam_init)
    else:
        for h in range(n_q):
            o_ref[0, :, h * HEAD_DIM:(h + 1) * HEAD_DIM] = o[h * tq:(h + 1) * tq]


def _attn_call(q, k_lat, v_lat, k_ctx, v_ctx, *, n_groups, n_q, shared_k, dv, diff=False, lam=None, gain=None,
               lam_init=0.0, tq=256, kv_chunk=1024):
    B, Tq, _ = q.shape
    C = k_ctx.shape[1]
    tq = min(tq, Tq)
    has_lat = k_lat is not None
    qw = n_q * HEAD_DIM
    kw = HEAD_DIM if shared_k else qw
    in_specs = [pl.BlockSpec((1, tq, qw), lambda b, g, t: (b, t, g))]
    args = [q]
    if has_lat:
        T = k_lat.shape[1]
        in_specs += [pl.BlockSpec((1, T, kw), lambda b, g, t: (b, 0, g)),
                     pl.BlockSpec((1, T, dv), lambda b, g, t: (b, 0, g))]
        args += [k_lat, v_lat]
    in_specs += [pl.BlockSpec((1, C, kw), lambda b, g, t: (b, 0, g)),
                 pl.BlockSpec((1, C, dv), lambda b, g, t: (b, 0, g))]
    args += [k_ctx, v_ctx]
    if diff:
        in_specs += [pl.BlockSpec((1, dv), lambda b, g, t: (0, 0))] * 2
        args += [lam, gain]
    ow = dv if diff else qw
    return pl.pallas_call(
        functools.partial(_attn_kernel, n_q=n_q, shared_k=shared_k, has_lat=has_lat, diff=diff, lam_init=lam_init,
                          kv_chunk=min(kv_chunk, k_lat.shape[1]) if has_lat else 0),
        out_shape=_sds((B, Tq, n_groups * ow), F32), grid=(B, n_groups, Tq // tq),
        in_specs=in_specs, out_specs=pl.BlockSpec((1, tq, ow), lambda b, g, t: (b, t, g)),
        compiler_params=_params("parallel", "parallel", "arbitrary"), name="attn",
    )(*args)


def _shift_rows(x3, d, reverse):
    if d == 0:
        return x3
    s = x3.shape[1]
    return pltpu.roll(x3, (s - d) if reverse else d, 1)


def _shift_blocks(x3, d, reverse, fill):
    pad = jnp.full((d,) + x3.shape[1:], fill, x3.dtype)
    if reverse:
        return jnp.concatenate([x3[d:], pad], axis=0)
    return jnp.concatenate([pad, x3[:x3.shape[0] - d]], axis=0)


def _gla_chunk(q_ref, z_ref, v_ref, o_ref, st_scr, d, r, lb, reverse):
    n, s = HG_CHUNK, HG_SUB
    nb = n // s
    rows = pl.ds(r, n)
    shape3 = (nb, s, HEAD_DIM)
    qraw = q_ref[0, rows, :]
    z = z_ref[0, rows, :]
    v = v_ref[0, rows, :]
    q3 = (qraw * _sig(qraw) * (HEAD_DIM ** -0.5)).reshape(shape3)
    k3 = ((1.0 - lb) * _sig(-z)).reshape(shape3)
    g3 = jnp.log(lb + (1.0 - lb) * _sig(z)).reshape(shape3)
    v3 = v.reshape(shape3)
    v16 = v.astype(BF16)
    row = lax.broadcasted_iota(jnp.int32, shape3, 1)

    def same_block(dd):
        return (row < s - dd) if reverse else (row >= dd)

    w3 = g3
    for sh in (1, 2, 4):
        w3 = w3 + jnp.where(same_block(sh), _shift_rows(w3, sh, reverse), 0.0)
    tot = w3[:, :1, :] if reverse else w3[:, s - 1:, :]
    u3 = jnp.minimum(tot - w3, 0.0)
    order = range(nb - 1, -1, -1) if reverse else range(nb)
    before, after = [None] * nb, [None] * nb
    acc = jnp.zeros((1, 1, HEAD_DIM), F32)
    for blk in order:
        before[blk] = acc
        acc = acc + tot[blk:blk + 1]
    total = acc
    acc = jnp.zeros((1, 1, HEAD_DIM), F32)
    for blk in reversed(order):
        after[blk] = acc
        acc = acc + tot[blk:blk + 1]
    before3 = jnp.concatenate(before, axis=0)
    after3 = jnp.concatenate(after, axis=0)

    e_tot = jnp.exp(tot)
    qo = q3 * jnp.exp(w3)
    ko = k3 * jnp.exp(u3)
    st = st_scr[d]

    slabs, cur = [qo], qo
    for dist in range(2, nb):
        cur = cur * _shift_blocks(e_tot, dist - 1, reverse, 1.0)
        slabs.append(cur)
    qs = jnp.concatenate(slabs, axis=0).reshape((nb - 1) * n, HEAD_DIM).astype(BF16)
    sc = _dot_nt(qs, ko.reshape(n, HEAD_DIM).astype(BF16))
    ib = lax.broadcasted_iota(jnp.int32, (n, n), 0) // s
    jb = lax.broadcasted_iota(jnp.int32, (n, n), 1) // s
    bdist = (jb - ib) if reverse else (ib - jb)
    smat = jnp.zeros((n, n), F32)
    for dist in range(1, nb):
        smat = jnp.where(bdist == dist, sc[(dist - 1) * n:dist * n], smat)
    out = _dot(smat.astype(BF16), v16)
    out = out + _dot_nt((qo * jnp.exp(before3)).reshape(n, HEAD_DIM).astype(BF16), st.astype(BF16))

    prods = []
    for dd in range(s):
        e = jnp.exp(jnp.minimum(w3 - _shift_rows(w3, dd, reverse), 0.0))
        prods.append(jnp.where(same_block(dd), q3 * _shift_rows(k3, dd, reverse) * e, 0.0))
    pst = jnp.concatenate(prods, axis=0).reshape(s * n, HEAD_DIM).astype(BF16)
    sums = _dot(pst, jnp.ones((HEAD_DIM, HEAD_DIM), BF16)).reshape(s * nb, s, HEAD_DIM)
    diag = sums[:nb] * v3
    for dd in range(1, s):
        diag = diag + sums[dd * nb:(dd + 1) * nb] * _shift_rows(v3, dd, reverse)
    o_ref[0, rows, :] = out + diag.reshape(n, HEAD_DIM)

    kst = (ko * jnp.exp(after3)).reshape(n, HEAD_DIM).astype(BF16)
    st_scr[d] = st * jnp.exp(total.reshape(1, HEAD_DIM)) + _dot_tn(v16, kst)


def _gla_kernel(qf_ref, zf_ref, vf_ref, qb_ref, zb_ref, vb_ref, lb_ref, s0_ref,
                of_ref, ob_ref, sfin_ref, st_scr, *, tb):
    i = pl.program_id(2)

    @pl.when(i == 0)
    def _():
        st_scr[...] = s0_ref[...]

    lb = lb_ref[...]
    nchunk = tb // HG_CHUNK

    def body(c, carry):
        rf = pl.multiple_of(c * HG_CHUNK, HG_CHUNK)
        _gla_chunk(qf_ref, zf_ref, vf_ref, of_ref, st_scr, 0, rf, lb, False)
        rb = pl.multiple_of((nchunk - 1 - c) * HG_CHUNK, HG_CHUNK)
        _gla_chunk(qb_ref, zb_ref, vb_ref, ob_ref, st_scr, 1, rb, lb, True)
        return carry

    lax.fori_loop(0, nchunk, body, 0)

    @pl.when(i == pl.num_programs(2) - 1)
    def _():
        sfin_ref[...] = st_scr[...]


def _gla_call(p, lb, s0, tb=512):
    B, T, _ = p.shape
    tb = min(tb, T)
    nb = T // tb
    H = HG_HEADS
    blk = (1, tb, HEAD_DIM)
    fwd = lambda c0: pl.BlockSpec(blk, lambda b, h, i: (b, i, c0 + h))
    bwd = lambda c0: pl.BlockSpec(blk, lambda b, h, i: (b, nb - 1 - i, c0 + h))
    st = pl.BlockSpec((2, None, None, HEAD_DIM, HEAD_DIM), lambda b, h, i: (0, b, h, 0, 0))
    return pl.pallas_call(
        functools.partial(_gla_kernel, tb=tb),
        out_shape=[_sds((B, T, H * HEAD_DIM), F32), _sds((B, T, H * HEAD_DIM), F32),
                   _sds((2, B, H, HEAD_DIM, HEAD_DIM), F32)],
        grid=(B, H, nb),
        in_specs=[fwd(0), fwd(H), fwd(3 * H), bwd(0), bwd(2 * H), bwd(3 * H),
                  pl.BlockSpec((1, HEAD_DIM), lambda b, h, i: (0, h)), st],
        out_specs=[pl.BlockSpec(blk, lambda b, h, i: (b, i, h)),
                   pl.BlockSpec(blk, lambda b, h, i: (b, nb - 1 - i, h)), st],
        scratch_shapes=[pltpu.VMEM((2, HEAD_DIM, HEAD_DIM), F32)],
        compiler_params=_params("parallel", "parallel", "arbitrary"), name="gla",
    )(p, p, p, p, p, p, lb.reshape(1, H * HEAD_DIM), s0)


def _ret_chunk(q_ref, k_ref, v_ref, tabs, o_ref, st_scr, d, r, lg, reverse):
    n = RET_CHUNK
    rows = pl.ds(r, n)
    q = q_ref[0, rows, :]
    k = k_ref[0, rows, :]
    if tabs is not None:
        cos, sa, sb = (t[rows, :] for t in tabs)
        q = _rope(q, cos, sa, sb)
        k = _rope(k, cos, sa, sb)
    k = k * (HEAD_DIM ** -0.5)
    v = v_ref[0, rows, :].astype(BF16)
    ii = lax.broadcasted_iota(jnp.int32, (n, n), 0)
    jj = lax.broadcasted_iota(jnp.int32, (n, n), 1)
    rel = (jj - ii) if reverse else (ii - jj)
    decay = jnp.where(rel >= 0, jnp.exp(jnp.maximum(rel, 0).astype(F32) * lg), 0.0)
    pos = ((n - 1 - ii) if reverse else ii).astype(F32)
    q_dec = jnp.exp((pos + 1.0) * lg)
    k_dec = jnp.exp((n - 1.0 - pos) * lg)
    c_dec = jnp.exp(n * lg)
    st = st_scr[d]
    scores = _dot_nt(q.astype(BF16), k.astype(BF16)) * decay
    out = _dot(scores.astype(BF16), v) + _dot((q * q_dec).astype(BF16), st.astype(BF16))
    o_ref[0, rows, :] = out
    upd = _dot_tn((k * k_dec).astype(BF16), v)
    st_scr[d] = jnp.concatenate([c_dec, c_dec], axis=1) * st + upd


def _ret_kernel(*refs, tb, rotate):
    refs = list(refs)
    qf_ref, kf_ref, vf_ref, qb_ref, kb_ref, vb_ref = refs[:6]
    refs = refs[6:]
    tabs_f = tabs_b = None
    if rotate:
        tabs_f, tabs_b, refs = refs[:3], refs[3:6], refs[6:]
    lg_ref, s0_ref, of_ref, ob_ref, sfin_ref, st_scr = refs
    i = pl.program_id(2)

    @pl.when(i == 0)
    def _():
        st_scr[...] = s0_ref[...]

    nchunk = tb // RET_CHUNK
    lg_f = lg_ref[0]
    lg_b = lg_ref[1]

    def body(c, carry):
        rf = pl.multiple_of(c * RET_CHUNK, RET_CHUNK)
        _ret_chunk(qf_ref, kf_ref, vf_ref, tabs_f, of_ref, st_scr, 0, rf, lg_f, False)
        rb = pl.multiple_of((nchunk - 1 - c) * RET_CHUNK, RET_CHUNK)
        _ret_chunk(qb_ref, kb_ref, vb_ref, tabs_b, ob_ref, st_scr, 1, rb, lg_b, True)
        return carry

    lax.fori_loop(0, nchunk, body, 0)

    @pl.when(i == pl.num_programs(2) - 1)
    def _():
        sfin_ref[...] = st_scr[...]


def _ret_call(p, lg, s0, tables, tb=512):
    B, T, _ = p.shape
    tb = min(tb, T)
    nb = T // tb
    H = RET_HEADS
    rotate = tables is not None
    q0 = (GQA_HEADS + 2 * GQA_KV_HEADS)
    k0 = q0 + H
    v0 = (k0 + H) * HEAD_DIM // RET_DV
    qk = (1, tb, HEAD_DIM)
    vv = (1, tb, RET_DV)
    fwd = lambda shape, c0: pl.BlockSpec(shape, lambda b, h, i: (b, i, c0 + h))
    bwd = lambda shape, c0: pl.BlockSpec(shape, lambda b, h, i: (b, nb - 1 - i, c0 + h))
    in_specs = [fwd(qk, q0), fwd(qk, k0), fwd(vv, v0), bwd(qk, q0), bwd(qk, k0), bwd(vv, v0)]
    args = [p] * 6
    if rotate:
        in_specs += [pl.BlockSpec((tb, HEAD_DIM), lambda b, h, i: (i, 0))] * 3
        in_specs += [pl.BlockSpec((tb, HEAD_DIM), lambda b, h, i: (nb - 1 - i, 0))] * 3
        args += list(tables) * 2
    st = pl.BlockSpec((2, None, None, HEAD_DIM, RET_DV), lambda b, h, i: (0, b, h, 0, 0))
    in_specs += [pl.BlockSpec((2, None, 1, HEAD_DIM), lambda b, h, i: (0, h, 0, 0)), st]
    args += [lg, s0]
    return pl.pallas_call(
        functools.partial(_ret_kernel, tb=tb, rotate=rotate),
        out_shape=[_sds((B, T, H * RET_DV), F32), _sds((B, T, H * RET_DV), F32),
                   _sds((2, B, H, HEAD_DIM, RET_DV), F32)],
        grid=(B, H, nb), in_specs=in_specs,
        out_specs=[pl.BlockSpec(vv, lambda b, h, i: (b, i, h)),
                   pl.BlockSpec(vv, lambda b, h, i: (b, nb - 1 - i, h)), st],
        scratch_shapes=[pltpu.VMEM((2, HEAD_DIM, RET_DV), F32)],
        compiler_params=_params("parallel", "parallel", "arbitrary"), name="ret",
    )(*args)


def _outproj_even_kernel(of_ref, ob_ref, gate_ref, da_ref, x_ref, g5_ref, w_ref, hg_ref, o_ref, a_scr):
    W = HG_HEADS * HEAD_DIM
    for h in range(HG_HEADS):
        sl = slice(h * HEAD_DIM, (h + 1) * HEAD_DIM)
        o = of_ref[0, :, sl] + ob_ref[0, :, sl]
        a_scr[:, sl] = (_rms(o, hg_ref[...]) * _sig(gate_ref[0, :, sl])).astype(BF16)
    y = _dot(a_scr[...], w_ref[:W, :]) + _dot(da_ref[0].astype(BF16), w_ref[W:, :])
    o_ref[0] = x_ref[0] + g5_ref[0] * y


def _outproj_even_call(o_f, o_b, p, da, x, g5, w, hg_g, tm=256):
    B, T, D = x.shape
    tm = min(tm, T)
    W = HG_HEADS * HEAD_DIM
    half = pl.BlockSpec((1, tm, W), lambda b, t: (b, t, 0))
    full = pl.BlockSpec((1, tm, D), lambda b, t: (b, t, 0))
    return pl.pallas_call(
        _outproj_even_kernel, out_shape=_sds((B, T, D), F32), grid=(B, T // tm),
        in_specs=[half, half, pl.BlockSpec((1, tm, W), lambda b, t: (b, t, 4)), half, full,
                  pl.BlockSpec((1, 1, D), lambda b, t: (b, 0, 0)),
                  pl.BlockSpec(w.shape, lambda b, t: (0, 0)),
                  pl.BlockSpec((1, HEAD_DIM), lambda b, t: (0, 0))],
        out_specs=full, scratch_shapes=[pltpu.VMEM((tm, W), BF16)],
        compiler_params=_params("parallel", "parallel"), name="outproj_even",
    )(o_f, o_b, p, da, x, g5, w, hg_g.reshape(1, HEAD_DIM))


def _outproj_odd_kernel(og_ref, of_ref, ob_ref, gate_lo_ref, gate_hi_ref, x_ref, g5_ref, w_ref, rg_ref, o_ref, a_scr):
    W = GQA_HEADS * HEAD_DIM
    half_heads = RET_HEADS // 2
    for h in range(RET_HEADS):
        sl = slice(h * RET_DV, (h + 1) * RET_DV)
        o = of_ref[0, :, sl] + ob_ref[0, :, sl]
        gate_ref = gate_lo_ref if h < half_heads else gate_hi_ref
        gate = gate_ref[0, :, (h % half_heads) * RET_DV:(h % half_heads + 1) * RET_DV]
        a_scr[:, sl] = (_rms(o, rg_ref[...]) * (gate * _sig(gate))).astype(BF16)
    y = _dot(og_ref[0].astype(BF16), w_ref[:W, :]) + _dot(a_scr[...], w_ref[W:, :])
    o_ref[0] = x_ref[0] + g5_ref[0] * y


def _outproj_odd_call(o_g, o_f, o_b, p, x, g5, w, ret_g, tm=256):
    B, T, D = x.shape
    tm = min(tm, T)
    W = RET_HEADS * RET_DV
    gw = W // 2
    gate_blk, rem = divmod(p.shape[2] - W, gw)
    assert rem == 0
    half = pl.BlockSpec((1, tm, W), lambda b, t: (b, t, 0))
    full = pl.BlockSpec((1, tm, D), lambda b, t: (b, t, 0))
    return pl.pallas_call(
        _outproj_odd_kernel, out_shape=_sds((B, T, D), F32), grid=(B, T // tm),
        in_specs=[half, half, half, pl.BlockSpec((1, tm, gw), lambda b, t: (b, t, gate_blk)),
                  pl.BlockSpec((1, tm, gw), lambda b, t: (b, t, gate_blk + 1)), full,
                  pl.BlockSpec((1, 1, D), lambda b, t: (b, 0, 0)),
                  pl.BlockSpec(w.shape, lambda b, t: (0, 0)),
                  pl.BlockSpec((1, RET_DV), lambda b, t: (0, 0))],
        out_specs=full, scratch_shapes=[pltpu.VMEM((tm, W), BF16)],
        compiler_params=_params("parallel", "parallel"), name="outproj_odd",
    )(o_g, o_f, o_b, p, p, x, g5, w, ret_g.reshape(1, RET_DV))


def _rope_tables(n_tokens):
    rows = n_tokens // GRID_W
    row = jnp.repeat(jnp.arange(rows), GRID_W)
    col = jnp.tile(jnp.arange(GRID_W), rows)
    pos = jnp.stack([row, col], axis=-1).astype(F32)
    inv = ROPE_THETA ** (-jnp.arange(ROPE_QUARTER, dtype=F32) / ROPE_QUARTER)
    ang = pos[:, :, None] * inv
    cos, sin = jnp.cos(ang), jnp.sin(ang)
    zero = jnp.zeros_like(sin)
    lanes = lambda a, b: jnp.stack([a, b], axis=2).reshape(n_tokens, HEAD_DIM)
    return lanes(cos, cos), lanes(-sin, zero), lanes(zero, sin)


def _even_mixer(x, ctx, mx, mc, g, w_in, w_out, lb, hg_g, lam_p, da_g, layer_idx, tables, ctx_out):
    B = x.shape[0]
    lam_init = 0.8 - 0.6 * math.exp(-0.3 * layer_idx)
    lam = jnp.exp(jnp.sum(lam_p[0] * lam_p[1])) - jnp.exp(jnp.sum(lam_p[2] * lam_p[3])) + lam_init
    lam_row = jnp.full((1, DA_DV), lam, F32)
    da_gain = da_g.reshape(1, DA_DV)
    px = _inproj_call(x, mx[3], mx[4], g, w_in, tn=2048)
    pc = _inproj_call(ctx, mc[3], mc[4], g, w_in, tn=2048)

    s0 = jnp.zeros((2, B, HG_HEADS, HEAD_DIM, HEAD_DIM), F32)
    ocf, ocb, s_c = _gla_call(pc, lb, s0)
    oxf, oxb, _ = _gla_call(px, lb, s_c)

    qx, kx, vx = _prep_even_call(px, tables)
    qc, kc, vc = _prep_even_call(pc, None)
    attn = functools.partial(_attn_call, n_groups=DA_HEADS, n_q=2, shared_k=False, dv=DA_DV, diff=True,
                             lam=lam_row, gain=da_gain, lam_init=lam_init)
    da_x = attn(qx, kx, vx, kc, vc)
    x_new = _outproj_even_call(oxf, oxb, px, da_x, x, mx[5], w_out, hg_g)
    if not ctx_out:
        return x_new, None
    da_c = attn(qc, None, None, kc, vc)
    ctx_new = _outproj_even_call(ocf, ocb, pc, da_c, ctx, mc[5], w_out, hg_g)
    return x_new, ctx_new


def _odd_mixer(x, ctx, mx, mc, g, w_in, w_out, q_g, k_g, decay_logit, ret_g, tables, ctx_out):
    B = x.shape[0]
    px = _inproj_call(x, mx[3], mx[4], g, w_in, tn=w_in.shape[1] // 2)
    pc = _inproj_call(ctx, mc[3], mc[4], g, w_in, tn=w_in.shape[1] // 2)

    qx, kx, vx = _prep_odd_call(px, q_g, k_g, tables)
    qc, kc, vc = _prep_odd_call(pc, q_g, k_g, None)
    attn = functools.partial(_attn_call, n_groups=GQA_KV_HEADS, n_q=GQA_HEADS // GQA_KV_HEADS, shared_k=True,
                             dv=HEAD_DIM)
    og_x = attn(qx, kx, vx, kc, vc)

    log_gamma = jax.nn.log_sigmoid(decay_logit.astype(F32))
    lg = jnp.broadcast_to(log_gamma[:, :, None, None], (2, RET_HEADS, 1, HEAD_DIM))
    s0 = jnp.zeros((2, B, RET_HEADS, HEAD_DIM, RET_DV), F32)
    ocf, ocb, s_c = _ret_call(pc, lg, s0, None)
    oxf, oxb, _ = _ret_call(px, lg, s_c, tables)
    x_new = _outproj_odd_call(og_x, oxf, oxb, px, x, mx[5], w_out, ret_g)
    if not ctx_out:
        return x_new, None
    og_c = attn(qc, None, None, kc, vc)
    ctx_new = _outproj_odd_call(og_c, ocf, ocb, pc, ctx, mc[5], w_out, ret_g)
    return x_new, ctx_new


def kernel(x, c, ctx, c_ctx, mod_w, mod_b, norm_g, ffn_w13, ffn_w2, even_w_in, even_w_out, hgrn_lb_logits, hgrn_norm_g, diff_lambda, diff_norm_g, odd_w_in, odd_w_out, gqa_q_norm_g, gqa_k_norm_g, ret_decay_logit, ret_norm_g, final_norm_g):
    B, T, D = x.shape
    C = ctx.shape[1]
    depth = mod_w.shape[0]
    tables = _rope_tables(T)
    lower_bounds = jnp.cumsum(jax.nn.softmax(hgrn_lb_logits.astype(F32), axis=0), axis=0)

    rows = -(-(B + 1) // 8) * 8
    cc = jnp.zeros((rows, D), F32).at[:B].set(c).at[B].set(c_ctx)
    mod = _mod_call(cc, mod_w, mod_b)
    mod_x = mod[:, :B].reshape(depth, B, N_MOD, D).transpose(0, 2, 1, 3)[:, :, :, None, :]
    mod_c = mod[:, B].reshape(depth, N_MOD, 1, 1, D)

    w13 = ffn_w13.astype(BF16)
    w2 = ffn_w2.astype(BF16)
    even_in, even_out = even_w_in.astype(BF16), even_w_out.astype(BF16)
    odd_in, odd_out = odd_w_in.astype(BF16), odd_w_out.astype(BF16)

    ctx = ctx.reshape(1, B * C, D)
    for l in range(depth):
        last = l == depth - 1
        mx, mc = mod_x[l], mod_c[l]
        g = norm_g[l]
        x = _ffn_call(x, mx[0], mx[1], mx[2], g[0], w13, w2, l, 0)
        ctx = _ffn_call(ctx, mc[0], mc[1], mc[2], g[0], w13, w2, l, 0)
        ctx_b = ctx.reshape(B, C, D)
        mc_b = jnp.broadcast_to(mc, (N_MOD, B, 1, D))
        if l % 2 == 0:
            e = l // 2
            x, ctx_b = _even_mixer(x, ctx_b, mx, mc_b, g[1], even_in[e], even_out[e], lower_bounds[l], hgrn_norm_g[e],
                                   diff_lambda[e], diff_norm_g[e], l, tables, not last)
        else:
            o = l // 2
            x, ctx_b = _odd_mixer(x, ctx_b, mx, mc_b, g[1], odd_in[o], odd_out[o], gqa_q_norm_g[o], gqa_k_norm_g[o],
                                  ret_decay_logit[o], ret_norm_g[o], tables, not last)
        x = _ffn_call(x, mx[6], mx[7], mx[8], g[2], w13, w2, l, 1, final_g=final_norm_g if last else None)
        if not last:
            ctx = _ffn_call(ctx_b.reshape(1, B * C, D), mc[6], mc[7], mc[8], g[2], w13, w2, l, 1)
    return x
```

```python
import functools
import math

import numpy as np
import jax
import jax.numpy as jnp
from jax import lax
from jax.experimental import pallas as pl
from jax.experimental.pallas import tpu as pltpu

F32 = jnp.float32
BF16 = jnp.bfloat16

NORM_EPS = 1e-6
GRID_W = 64
HEAD_DIM = 128
ROPE_QUARTER = HEAD_DIM // 4
ROPE_THETA = 10000.0
N_MOD = 9

HG_HEADS = 8
HG_CHUNK = 64
HG_SUB = 8
DA_HEADS = 4
DA_DV = 256
GQA_HEADS = 8
GQA_KV_HEADS = 2
RET_HEADS = 4
RET_DV = 256
RET_CHUNK = 128
FFN_TF = 512
FFN_VMEM_LIMIT = 62 * 1024 * 1024
ROW_SLAB = 256
LOG2E = 1.4426950408889634

VMEM_LIMIT = 56 * 1024 * 1024


def _sds(shape, dtype):
    return jax.ShapeDtypeStruct(shape, dtype)


def _params(*sem, vmem=VMEM_LIMIT):
    return pltpu.CompilerParams(dimension_semantics=sem, vmem_limit_bytes=vmem)


def _sig(x):
    return 1.0 / (1.0 + jnp.exp(-x))


def _rms(x, g):
    return x * lax.rsqrt(jnp.mean(x * x, axis=-1, keepdims=True) + NORM_EPS) * g


def _dot(a, b):
    return jnp.dot(a, b, preferred_element_type=F32)


def _dot_nt(a, b):
    return lax.dot_general(a, b, (((1,), (1,)), ((), ())), preferred_element_type=F32)


def _dot_tn(a, b):
    return lax.dot_general(a, b, (((0,), (0,)), ((), ())), preferred_element_type=F32)


def _rope(x, cos, sa, sb):
    return x * cos + pltpu.roll(x, 96, 1) * sa + pltpu.roll(x, 32, 1) * sb


def _modulate_into(h_scr, x_ref, g_ref, sc_ref, sh_ref):
    for r in range(0, x_ref.shape[1], ROW_SLAB):
        rows = slice(r, r + ROW_SLAB)
        h = _rms(x_ref[0, rows, :], g_ref[...]) * (1.0 + sc_ref[0]) + sh_ref[0]
        h_scr[rows, :] = h.astype(BF16)


def _mod_kernel(c_ref, w_ref, b_ref, o_ref):
    c = c_ref[...]
    a = (c * _sig(c)).astype(BF16)
    o_ref[0] = _dot(a, w_ref[0].astype(BF16)) + b_ref[0]


def _mod_call(cc, mod_w, mod_b):
    L, D, N = mod_w.shape
    R = cc.shape[0]
    tn = 1024
    return pl.pallas_call(
        _mod_kernel, out_shape=_sds((L, R, N), F32), grid=(L, N // tn),
        in_specs=[pl.BlockSpec((R, D), lambda l, j: (0, 0)),
                  pl.BlockSpec((1, D, tn), lambda l, j: (l, 0, j)),
                  pl.BlockSpec((1, 1, tn), lambda l, j: (l, 0, j))],
        out_specs=pl.BlockSpec((1, R, tn), lambda l, j: (l, 0, j)),
        compiler_params=_params("parallel", "parallel"), name="mod",
    )(cc, mod_w, mod_b.reshape(L, 1, N))


def _ffn_kernel(*refs, final):
    if final:
        x_ref, sh_ref, sc_ref, gt_ref, g_ref, w1_ref, w3_ref, w2_ref, fg_ref, o_ref, h_scr = refs
    else:
        x_ref, sh_ref, sc_ref, gt_ref, g_ref, w1_ref, w3_ref, w2_ref, o_ref, h_scr = refs
    j = pl.program_id(2)

    @pl.when(j == 0)
    def _():
        _modulate_into(h_scr, x_ref, g_ref, sc_ref, sh_ref)
        o_ref[...] = jnp.zeros_like(o_ref)

    h = h_scr[...]
    a = _dot(h, w1_ref[...])
    b = _dot(h, w3_ref[...])
    u = (a * _sig(a) * b).astype(BF16)
    o_ref[0] += _dot(u, w2_ref[...])

    @pl.when(j == pl.num_programs(2) - 1)
    def _():
        for r in range(0, x_ref.shape[1], ROW_SLAB):
            rows = slice(r, r + ROW_SLAB)
            out = x_ref[0, rows, :] + 0.5 * gt_ref[0] * o_ref[0, rows, :]
            if final:
                out = _rms(out, fg_ref[...])
            o_ref[0, rows, :] = out


def _ffn_call(x, sh, sc, gt, g, w13, w2, l, i, final_g=None, tm=1024, tf=FFN_TF):
    B, T, D = x.shape
    nf = w2.shape[2] // tf
    tm = min(tm, T)
    vec = pl.BlockSpec((1, 1, D), lambda b, t, j: (b, 0, 0))
    row = pl.BlockSpec((1, D), lambda b, t, j: (0, 0))
    in_specs = [pl.BlockSpec((1, tm, D), lambda b, t, j: (b, t, 0)), vec, vec, vec, row,
                pl.BlockSpec((None, None, D, tf), lambda b, t, j: (l, i, 0, j)),
                pl.BlockSpec((None, None, D, tf), lambda b, t, j: (l, i, 0, nf + j)),
                pl.BlockSpec((None, None, tf, D), lambda b, t, j: (l, i, j, 0))]
    args = [x, sh, sc, gt, g.reshape(1, D), w13, w13, w2]
    if final_g is not None:
        in_specs.append(row)
        args.append(final_g.reshape(1, D))
    return pl.pallas_call(
        functools.partial(_ffn_kernel, final=final_g is not None),
        out_shape=_sds((B, T, D), F32), grid=(B, T // tm, nf),
        in_specs=in_specs, out_specs=pl.BlockSpec((1, tm, D), lambda b, t, j: (b, t, 0)),
        scratch_shapes=[pltpu.VMEM((tm, D), BF16)],
        compiler_params=_params("parallel", "parallel", "arbitrary", vmem=FFN_VMEM_LIMIT), name="ffn",
    )(*args)


def _inproj_kernel(x_ref, sh_ref, sc_ref, g_ref, w_ref, o_ref, h_scr):
    @pl.when(pl.program_id(2) == 0)
    def _():
        _modulate_into(h_scr, x_ref, g_ref, sc_ref, sh_ref)

    o_ref[0] = _dot(h_scr[...], w_ref[...])


def _inproj_call(x, sh, sc, g, w, tn, tm=1024):
    B, T, D = x.shape
    N = w.shape[1]
    tm = min(tm, T)
    vec = pl.BlockSpec((1, 1, D), lambda b, t, j: (b, 0, 0))
    return pl.pallas_call(
        _inproj_kernel, out_shape=_sds((B, T, N), F32), grid=(B, T // tm, N // tn),
        in_specs=[pl.BlockSpec((1, tm, D), lambda b, t, j: (b, t, 0)), vec, vec,
                  pl.BlockSpec((1, D), lambda b, t, j: (0, 0)),
                  pl.BlockSpec((D, tn), lambda b, t, j: (0, j))],
        out_specs=pl.BlockSpec((1, tm, tn), lambda b, t, j: (b, t, j)),
        scratch_shapes=[pltpu.VMEM((tm, D), BF16)],
        compiler_params=_params("parallel", "parallel", "arbitrary"), name="inproj",
    )(x, sh, sc, g.reshape(1, D), w)


def _prep_even_kernel(*refs, rotate):
    if rotate:
        q_ref, k_ref, v_ref, cos_ref, sa_ref, sb_ref, qo_ref, ko_ref, vo_ref = refs
        cos, sa, sb = cos_ref[...], sa_ref[...], sb_ref[...]
    else:
        q_ref, k_ref, v_ref, qo_ref, ko_ref, vo_ref = refs
    scale = HEAD_DIM ** -0.5 * LOG2E
    for h in range(2 * DA_HEADS):
        sl = slice(h * HEAD_DIM, (h + 1) * HEAD_DIM)
        q = q_ref[0, :, sl]
        k = k_ref[0, :, sl]
        if rotate:
            q = _rope(q, cos, sa, sb)
            k = _rope(k, cos, sa, sb)
        qo_ref[0, :, sl] = (q * scale).astype(BF16)
        ko_ref[0, :, sl] = k.astype(BF16)
    vo_ref[0] = v_ref[0].astype(BF16)


def _prep_even_call(p, tables, tp=512):
    B, T, _ = p.shape
    tp = min(tp, T)
    W = 2 * DA_HEADS * HEAD_DIM
    rotate = tables is not None
    in_specs = [pl.BlockSpec((1, tp, W), lambda b, t: (b, t, 5)),
                pl.BlockSpec((1, tp, W), lambda b, t: (b, t, 6)),
                pl.BlockSpec((1, tp, W), lambda b, t: (b, t, 7))]
    args = [p, p, p]
    if rotate:
        in_specs += [pl.BlockSpec((tp, HEAD_DIM), lambda b, t: (t, 0))] * 3
        args += list(tables)
    out = pl.BlockSpec((1, tp, W), lambda b, t: (b, t, 0))
    return pl.pallas_call(
        functools.partial(_prep_even_kernel, rotate=rotate),
        out_shape=[_sds((B, T, W), BF16)] * 3, grid=(B, T // tp),
        in_specs=in_specs, out_specs=[out, out, out],
        compiler_params=_params("parallel", "parallel"), name="prep_even",
    )(*args)


def _prep_odd_kernel(*refs, rotate):
    if rotate:
        q_ref, k_ref, v_ref, qg_ref, kg_ref, cos_ref, sa_ref, sb_ref, qo_ref, ko_ref, vo_ref = refs
        cos, sa, sb = cos_ref[...], sa_ref[...], sb_ref[...]
    else:
        q_ref, k_ref, v_ref, qg_ref, kg_ref, qo_ref, ko_ref, vo_ref = refs
    scale = HEAD_DIM ** -0.5 * LOG2E
    for h in range(GQA_HEADS):
        sl = slice(h * HEAD_DIM, (h + 1) * HEAD_DIM)
        q = _rms(q_ref[0, :, sl], qg_ref[...])
        if rotate:
            q = _rope(q, cos, sa, sb)
        qo_ref[0, :, sl] = (q * scale).astype(BF16)
    for h in range(GQA_KV_HEADS):
        sl = slice(h * HEAD_DIM, (h + 1) * HEAD_DIM)
        k = _rms(k_ref[0, :, sl], kg_ref[...])
        if rotate:
            k = _rope(k, cos, sa, sb)
        ko_ref[0, :, sl] = k.astype(BF16)
    vo_ref[0] = v_ref[0].astype(BF16)


def _prep_odd_call(p, q_g, k_g, tables, tp=512):
    B, T, _ = p.shape
    tp = min(tp, T)
    QW = GQA_HEADS * HEAD_DIM
    KW = GQA_KV_HEADS * HEAD_DIM
    rotate = tables is not None
    gain = pl.BlockSpec((1, HEAD_DIM), lambda b, t: (0, 0))
    in_specs = [pl.BlockSpec((1, tp, QW), lambda b, t: (b, t, 0)),
                pl.BlockSpec((1, tp, KW), lambda b, t: (b, t, QW // KW)),
                pl.BlockSpec((1, tp, KW), lambda b, t: (b, t, QW // KW + 1)), gain, gain]
    args = [p, p, p, q_g.reshape(1, HEAD_DIM), k_g.reshape(1, HEAD_DIM)]
    if rotate:
        in_specs += [pl.BlockSpec((tp, HEAD_DIM), lambda b, t: (t, 0))] * 3
        args += list(tables)
    return pl.pallas_call(
        functools.partial(_prep_odd_kernel, rotate=rotate),
        out_shape=[_sds((B, T, QW), BF16), _sds((B, T, KW), BF16), _sds((B, T, KW), BF16)],
        grid=(B, T // tp), in_specs=in_specs,
        out_specs=[pl.BlockSpec((1, tp, QW), lambda b, t: (b, t, 0)),
                   pl.BlockSpec((1, tp, KW), lambda b, t: (b, t, 0)),
                   pl.BlockSpec((1, tp, KW), lambda b, t: (b, t, 0))],
        compiler_params=_params("parallel", "parallel"), name="prep_odd",
    )(*args)


def _attn_kernel(*refs, n_q, shared_k, has_lat, diff, lam_init, kv_chunk):
    refs = list(refs)
    q_ref = refs.pop(0)
    if has_lat:
        kl_ref, vl_ref = refs.pop(0), refs.pop(0)
    kc_ref, vc_ref = refs.pop(0), refs.pop(0)
    if diff:
        lam_ref, g_ref = refs.pop(0), refs.pop(0)
    o_ref = refs.pop(0)
    tq = q_ref.shape[1]
    heads = [q_ref[0, :, h * HEAD_DIM:(h + 1) * HEAD_DIM] for h in range(n_q)]
    if shared_k:
        heads = [jnp.concatenate(heads, axis=0)]
    pieces = [(kc_ref, vc_ref, 0, kc_ref.shape[1])]
    if has_lat:
        pieces += [(kl_ref, vl_ref, c * kv_chunk, kv_chunk) for c in range(kl_ref.shape[1] // kv_chunk)]
    m = den = acc = None
    for k_ref, v_ref, start, size in pieces:
        rows = slice(start, start + size)
        s = jnp.concatenate([_dot_nt(qh, k_ref[0, rows, h * HEAD_DIM:(h + 1) * HEAD_DIM])
                             for h, qh in enumerate(heads)], axis=0)
        m_new = jnp.max(s, axis=-1, keepdims=True)
        if m is not None:
            m_new = jnp.maximum(m, m_new)
        p = jnp.exp2(s - m_new)
        part = p[:, :HEAD_DIM]
        for c in range(1, size // HEAD_DIM):
            part = part + p[:, c * HEAD_DIM:(c + 1) * HEAD_DIM]
        pv = _dot(p.astype(BF16), v_ref[0, rows, :])
        if m is None:
            den, acc = part, pv
        else:
            a = jnp.exp2(m - m_new)
            den = a * den + part
            acc = a * acc + pv
        m = m_new
    o = acc / jnp.sum(den, axis=-1, keepdims=True)
    if diff:
        d = o[:tq] - lam_ref[...] * o[tq:]
        o_ref[0] = _rms(d, g_ref[...]) * (1.0 - lam_init)
    else:
        for h in range(n_q):
            o_ref[0, :, h * HEAD_DIM:(h + 1) * HEAD_DIM] = o[h * tq:(h + 1) * tq]


def _attn_call(q, k_lat, v_lat, k_ctx, v_ctx, *, n_groups, n_q, shared_k, dv, diff=False, lam=None, gain=None,
               lam_init=0.0, tq=256, kv_chunk=1024):
    B, Tq, _ = q.shape
    C = k_ctx.shape[1]
    tq = min(tq, Tq)
    has_lat = k_lat is not None
    qw = n_q * HEAD_DIM
    kw = HEAD_DIM if shared_k else qw
    in_specs = [pl.BlockSpec((1, tq, qw), lambda b, g, t: (b, t, g))]
    args = [q]
    if has_lat:
        T = k_lat.shape[1]
        in_specs += [pl.BlockSpec((1, T, kw), lambda b, g, t: (b, 0, g)),
                     pl.BlockSpec((1, T, dv), lambda b, g, t: (b, 0, g))]
        args += [k_lat, v_lat]
    in_specs += [pl.BlockSpec((1, C, kw), lambda b, g, t: (b, 0, g)),
                 pl.BlockSpec((1, C, dv), lambda b, g, t: (b, 0, g))]
    args += [k_ctx, v_ctx]
    if diff:
        in_specs += [pl.BlockSpec((1, dv), lambda b, g, t: (0, 0))] * 2
        args += [lam, gain]
    ow = dv if diff else qw
    return pl.pallas_call(
        functools.partial(_attn_kernel, n_q=n_q, shared_k=shared_k, has_lat=has_lat, diff=diff, lam_init=lam_init,
                          kv_chunk=min(kv_chunk, k_lat.shape[1]) if has_lat else 0),
        out_shape=_sds((B, Tq, n_groups * ow), F32), grid=(B, n_groups, Tq // tq),
        in_specs=in_specs, out_specs=pl.BlockSpec((1, tq, ow), lambda b, g, t: (b, t, g)),
        compiler_params=_params("parallel", "parallel", "arbitrary"), name="attn",
    )(*args)


def _shift_rows(x3, d, reverse):
    if d == 0:
        return x3
    s = x3.shape[1]
    return pltpu.roll(x3, (s - d) if reverse else d, 1)


def _shift_blocks(x3, d, reverse, fill):
    pad = jnp.full((d,) + x3.shape[1:], fill, x3.dtype)
    if reverse:
        return jnp.concatenate([x3[d:], pad], axis=0)
    return jnp.concatenate([pad, x3[:x3.shape[0] - d]], axis=0)


def _gla_chunk(q_ref, z_ref, v_ref, o_ref, st_scr, d, r, lb, reverse):
    n, s = HG_CHUNK, HG_SUB
    nb = n // s
    rows = pl.ds(r, n)
    shape3 = (nb, s, HEAD_DIM)
    qraw = q_ref[0, rows, :]
    z = z_ref[0, rows, :]
    v = v_ref[0, rows, :]
    q3 = (qraw * _sig(qraw) * (HEAD_DIM ** -0.5)).reshape(shape3)
    sz = _sig(z)
    k3 = ((1.0 - lb) * (1.0 - sz)).reshape(shape3)
    g3 = jnp.log(lb + (1.0 - lb) * sz).reshape(shape3)
    v3 = v.reshape(shape3)
    v16 = v.astype(BF16)
    row = lax.broadcasted_iota(jnp.int32, shape3, 1)

    def same_block(dd):
        return (row < s - dd) if reverse else (row >= dd)

    w3 = g3
    for sh in (1, 2, 4):
        w3 = w3 + jnp.where(same_block(sh), _shift_rows(w3, sh, reverse), 0.0)
    tot = w3[:, :1, :] if reverse else w3[:, s - 1:, :]
    u3 = jnp.minimum(tot - w3, 0.0)
    order = range(nb - 1, -1, -1) if reverse else range(nb)
    before, after = [None] * nb, [None] * nb
    acc = jnp.zeros((1, 1, HEAD_DIM), F32)
    for blk in order:
        before[blk] = acc
        acc = acc + tot[blk:blk + 1]
    total = acc
    acc = jnp.zeros((1, 1, HEAD_DIM), F32)
    for blk in reversed(order):
        after[blk] = acc
        acc = acc + tot[blk:blk + 1]
    before3 = jnp.concatenate(before, axis=0)
    after3 = jnp.concatenate(after, axis=0)

    e_tot = jnp.exp(tot)
    qo = q3 * jnp.exp(w3)
    ko = k3 * jnp.exp(u3)
    st = st_scr[d]

    slabs, cur = [qo], qo
    for dist in range(2, nb):
        cur = cur * _shift_blocks(e_tot, dist - 1, reverse, 1.0)
        slabs.append(cur)
    qs = jnp.concatenate(slabs, axis=0).reshape((nb - 1) * n, HEAD_DIM).astype(BF16)
    sc = _dot_nt(qs, ko.reshape(n, HEAD_DIM).astype(BF16))
    ib = lax.broadcasted_iota(jnp.int32, (n, n), 0) // s
    jb = lax.broadcasted_iota(jnp.int32, (n, n), 1) // s
    bdist = (jb - ib) if reverse else (ib - jb)
    smat = jnp.zeros((n, n), F32)
    for dist in range(1, nb):
        smat = jnp.where(bdist == dist, sc[(dist - 1) * n:dist * n], smat)
    out = _dot(smat.astype(BF16), v16)
    out = out + _dot_nt((qo * jnp.exp(before3)).reshape(n, HEAD_DIM).astype(BF16), st.astype(BF16))

    prods = []
    for dd in range(s):
        e = jnp.exp(w3 - _shift_rows(w3, dd, reverse))
        prods.append(jnp.where(same_block(dd), q3 * _shift_rows(k3, dd, reverse) * e, 0.0))
    pst = jnp.concatenate(prods, axis=0).reshape(s * n, HEAD_DIM).astype(BF16)
    sums = _dot(pst, jnp.ones((HEAD_DIM, HEAD_DIM), BF16)).reshape(s * nb, s, HEAD_DIM)
    diag = sums[:nb] * v3
    for dd in range(1, s):
        diag = diag + sums[dd * nb:(dd + 1) * nb] * _shift_rows(v3, dd, reverse)
    o_ref[0, rows, :] = out + diag.reshape(n, HEAD_DIM)

    kst = (ko * jnp.exp(after3)).reshape(n, HEAD_DIM).astype(BF16)
    st_scr[d] = st * jnp.exp(total.reshape(1, HEAD_DIM)) + _dot_tn(v16, kst)


def _gla_kernel(qf_ref, zf_ref, vf_ref, qb_ref, zb_ref, vb_ref, lb_ref, s0_ref,
                of_ref, ob_ref, sfin_ref, st_scr, *, tb):
    i = pl.program_id(2)

    @pl.when(i == 0)
    def _():
        st_scr[...] = s0_ref[...]

    lb = lb_ref[...]
    nchunk = tb // HG_CHUNK

    def body(c, carry):
        rf = pl.multiple_of(c * HG_CHUNK, HG_CHUNK)
        _gla_chunk(qf_ref, zf_ref, vf_ref, of_ref, st_scr, 0, rf, lb, False)
        rb = pl.multiple_of((nchunk - 1 - c) * HG_CHUNK, HG_CHUNK)
        _gla_chunk(qb_ref, zb_ref, vb_ref, ob_ref, st_scr, 1, rb, lb, True)
        return carry

    lax.fori_loop(0, nchunk, body, 0, unroll=2)

    @pl.when(i == pl.num_programs(2) - 1)
    def _():
        sfin_ref[...] = st_scr[...]


def _gla_call(p, lb, s0, tb=1024):
    B, T, _ = p.shape
    tb = min(tb, T)
    nb = T // tb
    H = HG_HEADS
    blk = (1, tb, HEAD_DIM)
    fwd = lambda c0: pl.BlockSpec(blk, lambda b, h, i: (b, i, c0 + h))
    bwd = lambda c0: pl.BlockSpec(blk, lambda b, h, i: (b, nb - 1 - i, c0 + h))
    st = pl.BlockSpec((2, None, None, HEAD_DIM, HEAD_DIM), lambda b, h, i: (0, b, h, 0, 0))
    return pl.pallas_call(
        functools.partial(_gla_kernel, tb=tb),
        out_shape=[_sds((B, T, H * HEAD_DIM), F32), _sds((B, T, H * HEAD_DIM), F32),
                   _sds((2, B, H, HEAD_DIM, HEAD_DIM), F32)],
        grid=(B, H, nb),
        in_specs=[fwd(0), fwd(H), fwd(3 * H), bwd(0), bwd(2 * H), bwd(3 * H),
                  pl.BlockSpec((1, HEAD_DIM), lambda b, h, i: (0, h)), st],
        out_specs=[pl.BlockSpec(blk, lambda b, h, i: (b, i, h)),
                   pl.BlockSpec(blk, lambda b, h, i: (b, nb - 1 - i, h)), st],
        scratch_shapes=[pltpu.VMEM((2, HEAD_DIM, HEAD_DIM), F32)],
        compiler_params=_params("parallel", "parallel", "arbitrary"), name="gla",
    )(p, p, p, p, p, p, lb.reshape(1, H * HEAD_DIM), s0)


def _ret_decays(lg, reverse):
    n = RET_CHUNK
    ii = lax.broadcasted_iota(jnp.int32, (n, n), 0)
    jj = lax.broadcasted_iota(jnp.int32, (n, n), 1)
    rel = (jj - ii) if reverse else (ii - jj)
    decay = jnp.where(rel >= 0, jnp.exp(jnp.maximum(rel, 0).astype(F32) * lg), 0.0)
    pos = ((n - 1 - ii) if reverse else ii).astype(F32)
    q_dec = jnp.exp((pos + 1.0) * lg)
    k_dec = jnp.exp((n - 1.0 - pos) * lg)
    c_dec = jnp.exp(n * lg)
    return decay, q_dec, k_dec, c_dec


def _ret_chunk(q_ref, k_ref, v_ref, tabs, o_ref, st_scr, d, r, decays):
    n = RET_CHUNK
    rows = pl.ds(r, n)
    q = q_ref[0, rows, :]
    k = k_ref[0, rows, :]
    if tabs is not None:
        cos, sa, sb = (t[rows, :] for t in tabs)
        q = _rope(q, cos, sa, sb)
        k = _rope(k, cos, sa, sb)
    k = k * (HEAD_DIM ** -0.5)
    v = v_ref[0, rows, :].astype(BF16)
    decay, q_dec, k_dec, c_dec = decays
    st = st_scr[d]
    scores = _dot_nt(q.astype(BF16), k.astype(BF16)) * decay
    out = _dot(scores.astype(BF16), v) + _dot((q * q_dec).astype(BF16), st.astype(BF16))
    o_ref[0, rows, :] = out
    upd = _dot_tn((k * k_dec).astype(BF16), v)
    st_scr[d] = jnp.concatenate([c_dec, c_dec], axis=1) * st + upd


def _ret_kernel(*refs, tb, rotate):
    refs = list(refs)
    qf_ref, kf_ref, vf_ref, qb_ref, kb_ref, vb_ref = refs[:6]
    refs = refs[6:]
    tabs_f = tabs_b = None
    if rotate:
        tabs_f, tabs_b, refs = refs[:3], refs[3:6], refs[6:]
    lg_ref, s0_ref, of_ref, ob_ref, sfin_ref, st_scr = refs
    i = pl.program_id(2)

    @pl.when(i == 0)
    def _():
        st_scr[...] = s0_ref[...]

    nchunk = tb // RET_CHUNK
    dec_f = _ret_decays(lg_ref[0], False)
    dec_b = _ret_decays(lg_ref[1], True)

    def body(c, carry):
        rf = pl.multiple_of(c * RET_CHUNK, RET_CHUNK)
        _ret_chunk(qf_ref, kf_ref, vf_ref, tabs_f, of_ref, st_scr, 0, rf, dec_f)
        rb = pl.multiple_of((nchunk - 1 - c) * RET_CHUNK, RET_CHUNK)
        _ret_chunk(qb_ref, kb_ref, vb_ref, tabs_b, ob_ref, st_scr, 1, rb, dec_b)
        return carry

    lax.fori_loop(0, nchunk, body, 0)

    @pl.when(i == pl.num_programs(2) - 1)
    def _():
        sfin_ref[...] = st_scr[...]


def _ret_call(p, lg, s0, tables, tb=1024):
    B, T, _ = p.shape
    tb = min(tb, T)
    nb = T // tb
    H = RET_HEADS
    rotate = tables is not None
    q0 = (GQA_HEADS + 2 * GQA_KV_HEADS)
    k0 = q0 + H
    v0 = (k0 + H) * HEAD_DIM // RET_DV
    qk = (1, tb, HEAD_DIM)
    vv = (1, tb, RET_DV)
    fwd = lambda shape, c0: pl.BlockSpec(shape, lambda b, h, i: (b, i, c0 + h))
    bwd = lambda shape, c0: pl.BlockSpec(shape, lambda b, h, i: (b, nb - 1 - i, c0 + h))
    in_specs = [fwd(qk, q0), fwd(qk, k0), fwd(vv, v0), bwd(qk, q0), bwd(qk, k0), bwd(vv, v0)]
    args = [p] * 6
    if rotate:
        in_specs += [pl.BlockSpec((tb, HEAD_DIM), lambda b, h, i: (i, 0))] * 3
        in_specs += [pl.BlockSpec((tb, HEAD_DIM), lambda b, h, i: (nb - 1 - i, 0))] * 3
        args += list(tables) * 2
    st = pl.BlockSpec((2, None, None, HEAD_DIM, RET_DV), lambda b, h, i: (0, b, h, 0, 0))
    in_specs += [pl.BlockSpec((2, None, 1, HEAD_DIM), lambda b, h, i: (0, h, 0, 0)), st]
    args += [lg, s0]
    return pl.pallas_call(
        functools.partial(_ret_kernel, tb=tb, rotate=rotate),
        out_shape=[_sds((B, T, H * RET_DV), F32), _sds((B, T, H * RET_DV), F32),
                   _sds((2, B, H, HEAD_DIM, RET_DV), F32)],
        grid=(B, H, nb), in_specs=in_specs,
        out_specs=[pl.BlockSpec(vv, lambda b, h, i: (b, i, h)),
                   pl.BlockSpec(vv, lambda b, h, i: (b, nb - 1 - i, h)), st],
        scratch_shapes=[pltpu.VMEM((2, HEAD_DIM, RET_DV), F32)],
        compiler_params=_params("parallel", "parallel", "arbitrary"), name="ret",
    )(*args)


def _outproj_even_kernel(of_ref, ob_ref, gate_ref, da_ref, x_ref, g5_ref, w_ref, hg_ref, o_ref, a_scr):
    W = HG_HEADS * HEAD_DIM
    for h in range(HG_HEADS):
        sl = slice(h * HEAD_DIM, (h + 1) * HEAD_DIM)
        o = of_ref[0, :, sl] + ob_ref[0, :, sl]
        a_scr[:, sl] = (_rms(o, hg_ref[...]) * _sig(gate_ref[0, :, sl])).astype(BF16)
    y = _dot(a_scr[...], w_ref[:W, :]) + _dot(da_ref[0].astype(BF16), w_ref[W:, :])
    o_ref[0] = x_ref[0] + g5_ref[0] * y


def _outproj_even_call(o_f, o_b, p, da, x, g5, w, hg_g, tm=256):
    B, T, D = x.shape
    tm = min(tm, T)
    W = HG_HEADS * HEAD_DIM
    half = pl.BlockSpec((1, tm, W), lambda b, t: (b, t, 0))
    full = pl.BlockSpec((1, tm, D), lambda b, t: (b, t, 0))
    return pl.pallas_call(
        _outproj_even_kernel, out_shape=_sds((B, T, D), F32), grid=(B, T // tm),
        in_specs=[half, half, pl.BlockSpec((1, tm, W), lambda b, t: (b, t, 4)), half, full,
                  pl.BlockSpec((1, 1, D), lambda b, t: (b, 0, 0)),
                  pl.BlockSpec(w.shape, lambda b, t: (0, 0)),
                  pl.BlockSpec((1, HEAD_DIM), lambda b, t: (0, 0))],
        out_specs=full, scratch_shapes=[pltpu.VMEM((tm, W), BF16)],
        compiler_params=_params("parallel", "parallel"), name="outproj_even",
    )(o_f, o_b, p, da, x, g5, w, hg_g.reshape(1, HEAD_DIM))


def _outproj_odd_kernel(og_ref, of_ref, ob_ref, gate_lo_ref, gate_hi_ref, x_ref, g5_ref, w_ref, rg_ref, o_ref, a_scr):
    W = GQA_HEADS * HEAD_DIM
    half_heads = RET_HEADS // 2
    for h in range(RET_HEADS):
        sl = slice(h * RET_DV, (h + 1) * RET_DV)
        o = of_ref[0, :, sl] + ob_ref[0, :, sl]
        gate_ref = gate_lo_ref if h < half_heads else gate_hi_ref
        gate = gate_ref[0, :, (h % half_heads) * RET_DV:(h % half_heads + 1) * RET_DV]
        a_scr[:, sl] = (_rms(o, rg_ref[...]) * (gate * _sig(gate))).astype(BF16)
    y = _dot(og_ref[0].astype(BF16), w_ref[:W, :]) + _dot(a_scr[...], w_ref[W:, :])
    o_ref[0] = x_ref[0] + g5_ref[0] * y


def _outproj_odd_call(o_g, o_f, o_b, p, x, g5, w, ret_g, tm=256):
    B, T, D = x.shape
    tm = min(tm, T)
    W = RET_HEADS * RET_DV
    gw = W // 2
    gate_blk, rem = divmod(p.shape[2] - W, gw)
    assert rem == 0
    half = pl.BlockSpec((1, tm, W), lambda b, t: (b, t, 0))
    full = pl.BlockSpec((1, tm, D), lambda b, t: (b, t, 0))
    return pl.pallas_call(
        _outproj_odd_kernel, out_shape=_sds((B, T, D), F32), grid=(B, T // tm),
        in_specs=[half, half, half, pl.BlockSpec((1, tm, gw), lambda b, t: (b, t, gate_blk)),
                  pl.BlockSpec((1, tm, gw), lambda b, t: (b, t, gate_blk + 1)), full,
                  pl.BlockSpec((1, 1, D), lambda b, t: (b, 0, 0)),
                  pl.BlockSpec(w.shape, lambda b, t: (0, 0)),
                  pl.BlockSpec((1, RET_DV), lambda b, t: (0, 0))],
        out_specs=full, scratch_shapes=[pltpu.VMEM((tm, W), BF16)],
        compiler_params=_params("parallel", "parallel"), name="outproj_odd",
    )(o_g, o_f, o_b, p, p, x, g5, w, ret_g.reshape(1, RET_DV))


def _rope_tables(n_tokens):
    rows = n_tokens // GRID_W
    row = jnp.repeat(jnp.arange(rows), GRID_W)
    col = jnp.tile(jnp.arange(GRID_W), rows)
    pos = jnp.stack([row, col], axis=-1).astype(F32)
    inv = ROPE_THETA ** (-jnp.arange(ROPE_QUARTER, dtype=F32) / ROPE_QUARTER)
    ang = pos[:, :, None] * inv
    cos, sin = jnp.cos(ang), jnp.sin(ang)
    zero = jnp.zeros_like(sin)
    lanes = lambda a, b: jnp.stack([a, b], axis=2).reshape(n_tokens, HEAD_DIM)
    return lanes(cos, cos), lanes(-sin, zero), lanes(zero, sin)


def _even_mixer(x, ctx, mx, mc, g, w_in, w_out, lb, hg_g, lam_p, da_g, layer_idx, tables, ctx_out):
    B = x.shape[0]
    lam_init = 0.8 - 0.6 * math.exp(-0.3 * layer_idx)
    lam = jnp.exp(jnp.sum(lam_p[0] * lam_p[1])) - jnp.exp(jnp.sum(lam_p[2] * lam_p[3])) + lam_init
    lam_row = jnp.full((1, DA_DV), lam, F32)
    da_gain = da_g.reshape(1, DA_DV)
    px = _inproj_call(x, mx[3], mx[4], g, w_in, tn=1024)
    pc = _inproj_call(ctx, mc[3], mc[4], g, w_in, tn=1024)

    s0 = jnp.zeros((2, B, HG_HEADS, HEAD_DIM, HEAD_DIM), F32)
    ocf, ocb, s_c = _gla_call(pc, lb, s0)
    oxf, oxb, _ = _gla_call(px, lb, s_c)

    qx, kx, vx = _prep_even_call(px, tables)
    qc, kc, vc = _prep_even_call(pc, None)
    attn = functools.partial(_attn_call, n_groups=DA_HEADS, n_q=2, shared_k=False, dv=DA_DV, diff=True,
                             lam=lam_row, gain=da_gain, lam_init=lam_init, tq=512)
    da_x = attn(qx, kx, vx, kc, vc)
    x_new = _outproj_even_call(oxf, oxb, px, da_x, x, mx[5], w_out, hg_g)
    if not ctx_out:
        return x_new, None
    da_c = attn(qc, None, None, kc, vc)
    ctx_new = _outproj_even_call(ocf, ocb, pc, da_c, ctx, mc[5], w_out, hg_g)
    return x_new, ctx_new


def _odd_mixer(x, ctx, mx, mc, g, w_in, w_out, q_g, k_g, decay_logit, ret_g, tables, ctx_out):
    B = x.shape[0]
    px = _inproj_call(x, mx[3], mx[4], g, w_in, tn=w_in.shape[1] // 3)
    pc = _inproj_call(ctx, mc[3], mc[4], g, w_in, tn=w_in.shape[1] // 3)

    qx, kx, vx = _prep_odd_call(px, q_g, k_g, tables)
    qc, kc, vc = _prep_odd_call(pc, q_g, k_g, None)
    attn = functools.partial(_attn_call, n_groups=GQA_KV_HEADS, n_q=GQA_HEADS // GQA_KV_HEADS, shared_k=True,
                             dv=HEAD_DIM)
    og_x = attn(qx, kx, vx, kc, vc)

    log_gamma = jax.nn.log_sigmoid(decay_logit.astype(F32))
    lg = jnp.broadcast_to(log_gamma[:, :, None, None], (2, RET_HEADS, 1, HEAD_DIM))
    s0 = jnp.zeros((2, B, RET_HEADS, HEAD_DIM, RET_DV), F32)
    ocf, ocb, s_c = _ret_call(pc, lg, s0, None)
    oxf, oxb, _ = _ret_call(px, lg, s_c, tables)
    x_new = _outproj_odd_call(og_x, oxf, oxb, px, x, mx[5], w_out, ret_g)
    if not ctx_out:
        return x_new, None
    og_c = attn(qc, None, None, kc, vc)
    ctx_new = _outproj_odd_call(og_c, ocf, ocb, pc, ctx, mc[5], w_out, ret_g)
    return x_new, ctx_new


def kernel(x, c, ctx, c_ctx, mod_w, mod_b, norm_g, ffn_w13, ffn_w2, even_w_in, even_w_out, hgrn_lb_logits, hgrn_norm_g, diff_lambda, diff_norm_g, odd_w_in, odd_w_out, gqa_q_norm_g, gqa_k_norm_g, ret_decay_logit, ret_norm_g, final_norm_g):
    B, T, D = x.shape
    C = ctx.shape[1]
    depth = mod_w.shape[0]
    tables = _rope_tables(T)
    lower_bounds = jnp.cumsum(jax.nn.softmax(hgrn_lb_logits.astype(F32), axis=0), axis=0)

    rows = -(-(B + 1) // 8) * 8
    cc = jnp.zeros((rows, D), F32).at[:B].set(c).at[B].set(c_ctx)
    mod = _mod_call(cc, mod_w, mod_b)
    mod_x = mod[:, :B].reshape(depth, B, N_MOD, D).transpose(0, 2, 1, 3)[:, :, :, None, :]
    mod_c = mod[:, B].reshape(depth, N_MOD, 1, 1, D)

    w13 = ffn_w13.astype(BF16)
    w2 = ffn_w2.astype(BF16)
    even_in, even_out = even_w_in.astype(BF16), even_w_out.astype(BF16)
    odd_in, odd_out = odd_w_in.astype(BF16), odd_w_out.astype(BF16)

    ctx = ctx.reshape(1, B * C, D)
    for l in range(depth):
        last = l == depth - 1
        mx, mc = mod_x[l], mod_c[l]
        g = norm_g[l]
        x = _ffn_call(x, mx[0], mx[1], mx[2], g[0], w13, w2, l, 0)
        ctx = _ffn_call(ctx, mc[0], mc[1], mc[2], g[0], w13, w2, l, 0)
        ctx_b = ctx.reshape(B, C, D)
        mc_b = jnp.broadcast_to(mc, (N_MOD, B, 1, D))
        if l % 2 == 0:
            e = l // 2
            x, ctx_b = _even_mixer(x, ctx_b, mx, mc_b, g[1], even_in[e], even_out[e], lower_bounds[l], hgrn_norm_g[e],
                                   diff_lambda[e], diff_norm_g[e], l, tables, not last)
        else:
            o = l // 2
            x, ctx_b = _odd_mixer(x, ctx_b, mx, mc_b, g[1], odd_in[o], odd_out[o], gqa_q_norm_g[o], gqa_k_norm_g[o],
                                  ret_decay_logit[o], ret_norm_g[o], tables, not last)
        x = _ffn_call(x, mx[6], mx[7], mx[8], g[2], w13, w2, l, 1, final_g=final_norm_g if last else None)
        if not last:
            ctx = _ffn_call(ctx_b.reshape(1, B * C, D), mc[6], mc[7], mc[8], g[2], w13, w2, l, 1)
    return x
```

```python
import functools
import math

import numpy as np
import jax
import jax.numpy as jnp
from jax import lax
from jax.experimental import pallas as pl
from jax.experimental.pallas import tpu as pltpu

F32 = jnp.float32
BF16 = jnp.bfloat16

NORM_EPS = 1e-6
GRID_W = 64
HEAD_DIM = 128
ROPE_QUARTER = HEAD_DIM // 4
ROPE_THETA = 10000.0
N_MOD = 9

HG_HEADS = 8
HG_CHUNK = 64
HG_SUB = 8
DA_HEADS = 4
DA_DV = 256
GQA_HEADS = 8
GQA_KV_HEADS = 2
RET_HEADS = 4
RET_DV = 256
RET_CHUNK = 128
FFN_TF = 512
FFN_VMEM_LIMIT = 62 * 1024 * 1024
ROW_SLAB = 256
LOG2E = 1.4426950408889634

VMEM_LIMIT = 56 * 1024 * 1024


def _sds(shape, dtype):
    return jax.ShapeDtypeStruct(shape, dtype)


def _params(*sem, vmem=VMEM_LIMIT):
    return pltpu.CompilerParams(dimension_semantics=sem, vmem_limit_bytes=vmem)


def _sig(x):
    return 1.0 / (1.0 + jnp.exp(-x))


def _rms(x, g):
    return x * lax.rsqrt(jnp.mean(x * x, axis=-1, keepdims=True) + NORM_EPS) * g


def _dot(a, b):
    return jnp.dot(a, b, preferred_element_type=F32)


def _dot_nt(a, b):
    return lax.dot_general(a, b, (((1,), (1,)), ((), ())), preferred_element_type=F32)


def _dot_tn(a, b):
    return lax.dot_general(a, b, (((0,), (0,)), ((), ())), preferred_element_type=F32)


def _rope(x, cos, sa, sb):
    return x * cos + pltpu.roll(x, 96, 1) * sa + pltpu.roll(x, 32, 1) * sb


def _modulate_into(h_scr, x_ref, g_ref, sc_ref, sh_ref):
    gain = g_ref[...] * (1.0 + sc_ref[0])
    shift = sh_ref[0]
    for r in range(0, x_ref.shape[1], ROW_SLAB):
        rows = slice(r, r + ROW_SLAB)
        x = x_ref[0, rows, :]
        inv = lax.rsqrt(jnp.mean(x * x, axis=-1, keepdims=True) + NORM_EPS)
        h_scr[rows, :] = (x * inv * gain + shift).astype(BF16)


def _mod_kernel(c_ref, w_ref, b_ref, o_ref):
    c = c_ref[...]
    a = (c * _sig(c)).astype(BF16)
    o_ref[0] = _dot(a, w_ref[0].astype(BF16)) + b_ref[0]


def _mod_call(cc, mod_w, mod_b):
    L, D, N = mod_w.shape
    R = cc.shape[0]
    tn = 1024
    return pl.pallas_call(
        _mod_kernel, out_shape=_sds((L, R, N), F32), grid=(L, N // tn),
        in_specs=[pl.BlockSpec((R, D), lambda l, j: (0, 0)),
                  pl.BlockSpec((1, D, tn), lambda l, j: (l, 0, j)),
                  pl.BlockSpec((1, 1, tn), lambda l, j: (l, 0, j))],
        out_specs=pl.BlockSpec((1, R, tn), lambda l, j: (l, 0, j)),
        compiler_params=_params("parallel", "parallel"), name="mod",
    )(cc, mod_w, mod_b.reshape(L, 1, N))


def _ffn_kernel(*refs, final):
    if final:
        x_ref, sh_ref, sc_ref, gt_ref, g_ref, w1_ref, w3_ref, w2_ref, fg_ref, o_ref, h_scr = refs
    else:
        x_ref, sh_ref, sc_ref, gt_ref, g_ref, w1_ref, w3_ref, w2_ref, o_ref, h_scr = refs
    j = pl.program_id(2)

    @pl.when(j == 0)
    def _():
        _modulate_into(h_scr, x_ref, g_ref, sc_ref, sh_ref)
        o_ref[...] = jnp.zeros_like(o_ref)

    h = h_scr[...]
    a = _dot(h, w1_ref[...])
    b = _dot(h, w3_ref[...])
    u = (a * _sig(a) * b).astype(BF16)
    o_ref[0] += _dot(u, w2_ref[...])

    @pl.when(j == pl.num_programs(2) - 1)
    def _():
        for r in range(0, x_ref.shape[1], ROW_SLAB):
            rows = slice(r, r + ROW_SLAB)
            out = x_ref[0, rows, :] + 0.5 * gt_ref[0] * o_ref[0, rows, :]
            if final:
                out = _rms(out, fg_ref[...])
            o_ref[0, rows, :] = out


def _ffn_call(x, sh, sc, gt, g, w13, w2, l, i, final_g=None, tm=1024, tf=FFN_TF):
    B, T, D = x.shape
    nf = w2.shape[2] // tf
    tm = min(tm, T)
    vec = pl.BlockSpec((1, 1, D), lambda b, t, j: (b, 0, 0))
    row = pl.BlockSpec((1, D), lambda b, t, j: (0, 0))
    in_specs = [pl.BlockSpec((1, tm, D), lambda b, t, j: (b, t, 0)), vec, vec, vec, row,
                pl.BlockSpec((None, None, D, tf), lambda b, t, j: (l, i, 0, j)),
                pl.BlockSpec((None, None, D, tf), lambda b, t, j: (l, i, 0, nf + j)),
                pl.BlockSpec((None, None, tf, D), lambda b, t, j: (l, i, j, 0))]
    args = [x, sh, sc, gt, g.reshape(1, D), w13, w13, w2]
    if final_g is not None:
        in_specs.append(row)
        args.append(final_g.reshape(1, D))
    return pl.pallas_call(
        functools.partial(_ffn_kernel, final=final_g is not None),
        out_shape=_sds((B, T, D), F32), grid=(B, T // tm, nf),
        in_specs=in_specs, out_specs=pl.BlockSpec((1, tm, D), lambda b, t, j: (b, t, 0)),
        scratch_shapes=[pltpu.VMEM((tm, D), BF16)],
        compiler_params=_params("parallel", "parallel", "arbitrary", vmem=FFN_VMEM_LIMIT), name="ffn",
    )(*args)


def _inproj_kernel(x_ref, sh_ref, sc_ref, g_ref, w_ref, o_ref, h_scr):
    @pl.when(pl.program_id(2) == 0)
    def _():
        _modulate_into(h_scr, x_ref, g_ref, sc_ref, sh_ref)

    o_ref[0] = _dot(h_scr[...], w_ref[...])


def _inproj_call(x, sh, sc, g, w, tn, tm=1024):
    B, T, D = x.shape
    N = w.shape[1]
    tm = min(tm, T)
    vec = pl.BlockSpec((1, 1, D), lambda b, t, j: (b, 0, 0))
    return pl.pallas_call(
        _inproj_kernel, out_shape=_sds((B, T, N), F32), grid=(B, T // tm, N // tn),
        in_specs=[pl.BlockSpec((1, tm, D), lambda b, t, j: (b, t, 0)), vec, vec,
                  pl.BlockSpec((1, D), lambda b, t, j: (0, 0)),
                  pl.BlockSpec((D, tn), lambda b, t, j: (0, j))],
        out_specs=pl.BlockSpec((1, tm, tn), lambda b, t, j: (b, t, j)),
        scratch_shapes=[pltpu.VMEM((tm, D), BF16)],
        compiler_params=_params("parallel", "parallel", "arbitrary"), name="inproj",
    )(x, sh, sc, g.reshape(1, D), w)


def _prep_even_kernel(*refs, rotate):
    if rotate:
        q_ref, k_ref, v_ref, cos_ref, sa_ref, sb_ref, qo_ref, ko_ref, vo_ref = refs
        cos, sa, sb = cos_ref[...], sa_ref[...], sb_ref[...]
    else:
        q_ref, k_ref, v_ref, qo_ref, ko_ref, vo_ref = refs
    scale = HEAD_DIM ** -0.5 * LOG2E
    for h in range(2 * DA_HEADS):
        sl = slice(h * HEAD_DIM, (h + 1) * HEAD_DIM)
        q = q_ref[0, :, sl]
        k = k_ref[0, :, sl]
        if rotate:
            q = _rope(q, cos, sa, sb)
            k = _rope(k, cos, sa, sb)
        qo_ref[0, :, sl] = (q * scale).astype(BF16)
        ko_ref[0, :, sl] = k.astype(BF16)
    vo_ref[0] = v_ref[0].astype(BF16)


def _prep_even_call(p, tables, tp=1024):
    B, T, _ = p.shape
    tp = min(tp, T)
    W = 2 * DA_HEADS * HEAD_DIM
    rotate = tables is not None
    in_specs = [pl.BlockSpec((1, tp, W), lambda b, t: (b, t, 5)),
                pl.BlockSpec((1, tp, W), lambda b, t: (b, t, 6)),
                pl.BlockSpec((1, tp, W), lambda b, t: (b, t, 7))]
    args = [p, p, p]
    if rotate:
        in_specs += [pl.BlockSpec((tp, HEAD_DIM), lambda b, t: (t, 0))] * 3
        args += list(tables)
    out = pl.BlockSpec((1, tp, W), lambda b, t: (b, t, 0))
    return pl.pallas_call(
        functools.partial(_prep_even_kernel, rotate=rotate),
        out_shape=[_sds((B, T, W), BF16)] * 3, grid=(B, T // tp),
        in_specs=in_specs, out_specs=[out, out, out],
        compiler_params=_params("parallel", "parallel"), name="prep_even",
    )(*args)


def _prep_odd_kernel(*refs, rotate):
    if rotate:
        q_ref, k_ref, v_ref, qg_ref, kg_ref, cos_ref, sa_ref, sb_ref, qo_ref, ko_ref, vo_ref = refs
        cos, sa, sb = cos_ref[...], sa_ref[...], sb_ref[...]
    else:
        q_ref, k_ref, v_ref, qg_ref, kg_ref, qo_ref, ko_ref, vo_ref = refs
    scale = HEAD_DIM ** -0.5 * LOG2E
    for h in range(GQA_HEADS):
        sl = slice(h * HEAD_DIM, (h + 1) * HEAD_DIM)
        q = _rms(q_ref[0, :, sl], qg_ref[...])
        if rotate:
            q = _rope(q, cos, sa, sb)
        qo_ref[0, :, sl] = (q * scale).astype(BF16)
    for h in range(GQA_KV_HEADS):
        sl = slice(h * HEAD_DIM, (h + 1) * HEAD_DIM)
        k = _rms(k_ref[0, :, sl], kg_ref[...])
        if rotate:
            k = _rope(k, cos, sa, sb)
        ko_ref[0, :, sl] = k.astype(BF16)
    vo_ref[0] = v_ref[0].astype(BF16)


def _prep_odd_call(p, q_g, k_g, tables, tp=1024):
    B, T, _ = p.shape
    tp = min(tp, T)
    QW = GQA_HEADS * HEAD_DIM
    KW = GQA_KV_HEADS * HEAD_DIM
    rotate = tables is not None
    gain = pl.BlockSpec((1, HEAD_DIM), lambda b, t: (0, 0))
    in_specs = [pl.BlockSpec((1, tp, QW), lambda b, t: (b, t, 0)),
                pl.BlockSpec((1, tp, KW), lambda b, t: (b, t, QW // KW)),
                pl.BlockSpec((1, tp, KW), lambda b, t: (b, t, QW // KW + 1)), gain, gain]
    args = [p, p, p, q_g.reshape(1, HEAD_DIM), k_g.reshape(1, HEAD_DIM)]
    if rotate:
        in_specs += [pl.BlockSpec((tp, HEAD_DIM), lambda b, t: (t, 0))] * 3
        args += list(tables)
    return pl.pallas_call(
        functools.partial(_prep_odd_kernel, rotate=rotate),
        out_shape=[_sds((B, T, QW), BF16), _sds((B, T, KW), BF16), _sds((B, T, KW), BF16)],
        grid=(B, T // tp), in_specs=in_specs,
        out_specs=[pl.BlockSpec((1, tp, QW), lambda b, t: (b, t, 0)),
                   pl.BlockSpec((1, tp, KW), lambda b, t: (b, t, 0)),
                   pl.BlockSpec((1, tp, KW), lambda b, t: (b, t, 0))],
        compiler_params=_params("parallel", "parallel"), name="prep_odd",
    )(*args)


def _attn_kernel(*refs, n_q, shared_k, has_lat, diff, lam_init, kv_chunk):
    refs = list(refs)
    q_ref = refs.pop(0)
    if has_lat:
        kl_ref, vl_ref = refs.pop(0), refs.pop(0)
    kc_ref, vc_ref = refs.pop(0), refs.pop(0)
    if diff:
        lam_ref, g_ref = refs.pop(0), refs.pop(0)
    o_ref = refs.pop(0)
    tq = q_ref.shape[1]
    heads = [q_ref[0, :, h * HEAD_DIM:(h + 1) * HEAD_DIM] for h in range(n_q)]
    if shared_k:
        heads = [jnp.concatenate(heads, axis=0)]
    pieces = [(kc_ref, vc_ref, 0, kc_ref.shape[1])]
    if has_lat:
        pieces += [(kl_ref, vl_ref, c * kv_chunk, kv_chunk) for c in range(kl_ref.shape[1] // kv_chunk)]
    state = [None] * len(heads)
    for k_ref, v_ref, start, size in pieces:
        rows = slice(start, start + size)
        for h, qh in enumerate(heads):
            kh = 0 if shared_k else h
            s = _dot_nt(qh, k_ref[0, rows, kh * HEAD_DIM:(kh + 1) * HEAD_DIM])
            m_new = jnp.max(s, axis=-1, keepdims=True)
            if state[h] is not None:
                m, den, acc = state[h]
                m_new = jnp.maximum(m, m_new)
            p = jnp.exp2(s - m_new)
            part = p[:, :HEAD_DIM]
            for c in range(1, size // HEAD_DIM):
                part = part + p[:, c * HEAD_DIM:(c + 1) * HEAD_DIM]
            pv = _dot(p.astype(BF16), v_ref[0, rows, :])
            if state[h] is not None:
                a = jnp.exp2(m - m_new)
                part = a * den + part
                pv = a * acc + pv
            state[h] = (m_new, part, pv)
    outs = [acc / jnp.sum(den, axis=-1, keepdims=True) for _, den, acc in state]
    if diff:
        d = outs[0] - lam_ref[...] * outs[1]
        o_ref[0] = _rms(d, g_ref[...]) * (1.0 - lam_init)
    else:
        for h in range(n_q):
            o_ref[0, :, h * HEAD_DIM:(h + 1) * HEAD_DIM] = outs[0][h * tq:(h + 1) * tq]


def _attn_call(q, k_lat, v_lat, k_ctx, v_ctx, *, n_groups, n_q, shared_k, dv, diff=False, lam=None, gain=None,
               lam_init=0.0, tq=256, kv_chunk=1024):
    assert (diff and n_q == 2 and not shared_k) or (shared_k and not diff)
    B, Tq, _ = q.shape
    C = k_ctx.shape[1]
    tq = min(tq, Tq)
    has_lat = k_lat is not None
    qw = n_q * HEAD_DIM
    kw = HEAD_DIM if shared_k else qw
    in_specs = [pl.BlockSpec((1, tq, qw), lambda b, g, t: (b, t, g))]
    args = [q]
    if has_lat:
        T = k_lat.shape[1]
        in_specs += [pl.BlockSpec((1, T, kw), lambda b, g, t: (b, 0, g)),
                     pl.BlockSpec((1, T, dv), lambda b, g, t: (b, 0, g))]
        args += [k_lat, v_lat]
    in_specs += [pl.BlockSpec((1, C, kw), lambda b, g, t: (b, 0, g)),
                 pl.BlockSpec((1, C, dv), lambda b, g, t: (b, 0, g))]
    args += [k_ctx, v_ctx]
    if diff:
        in_specs += [pl.BlockSpec((1, dv), lambda b, g, t: (0, 0))] * 2
        args += [lam, gain]
    ow = dv if diff else qw
    return pl.pallas_call(
        functools.partial(_attn_kernel, n_q=n_q, shared_k=shared_k, has_lat=has_lat, diff=diff, lam_init=lam_init,
                          kv_chunk=min(kv_chunk, k_lat.shape[1]) if has_lat else 0),
        out_shape=_sds((B, Tq, n_groups * ow), F32), grid=(B, n_groups, Tq // tq),
        in_specs=in_specs, out_specs=pl.BlockSpec((1, tq, ow), lambda b, g, t: (b, t, g)),
        compiler_params=_params("parallel", "parallel", "arbitrary"), name="attn",
    )(*args)


def _shift_rows(x3, d, reverse):
    if d == 0:
        return x3
    s = x3.shape[1]
    return pltpu.roll(x3, (s - d) if reverse else d, 1)


def _shift_blocks(x3, d, reverse, fill):
    pad = jnp.full((d,) + x3.shape[1:], fill, x3.dtype)
    if reverse:
        return jnp.concatenate([x3[d:], pad], axis=0)
    return jnp.concatenate([pad, x3[:x3.shape[0] - d]], axis=0)


def _gla_chunk(q_ref, z_ref, v_ref, o_ref, st_scr, d, r, lb, reverse):
    n, s = HG_CHUNK, HG_SUB
    nb = n // s
    rows = pl.ds(r, n)
    shape3 = (nb, s, HEAD_DIM)
    qraw = q_ref[0, rows, :]
    z = z_ref[0, rows, :]
    v = v_ref[0, rows, :]
    q3 = (qraw * _sig(qraw) * (HEAD_DIM ** -0.5)).reshape(shape3)
    sz = _sig(z)
    k3 = ((1.0 - lb) * (1.0 - sz)).reshape(shape3)
    g3 = jnp.log2(lb + (1.0 - lb) * sz).reshape(shape3)
    v3 = v.reshape(shape3)
    v16 = v.astype(BF16)
    row = lax.broadcasted_iota(jnp.int32, shape3, 1)

    def same_block(dd):
        return (row < s - dd) if reverse else (row >= dd)

    w3 = g3
    for sh in (1, 2, 4):
        w3 = w3 + jnp.where(same_block(sh), _shift_rows(w3, sh, reverse), 0.0)
    tot = w3[:, :1, :] if reverse else w3[:, s - 1:, :]
    u3 = jnp.minimum(tot - w3, 0.0)
    order = range(nb - 1, -1, -1) if reverse else range(nb)
    before, after = [None] * nb, [None] * nb
    acc = jnp.zeros((1, 1, HEAD_DIM), F32)
    for blk in order:
        before[blk] = acc
        acc = acc + tot[blk:blk + 1]
    total = acc
    acc = jnp.zeros((1, 1, HEAD_DIM), F32)
    for blk in reversed(order):
        after[blk] = acc
        acc = acc + tot[blk:blk + 1]
    before3 = jnp.concatenate(before, axis=0)
    after3 = jnp.concatenate(after, axis=0)

    e_tot = jnp.exp2(tot)
    qo = q3 * jnp.exp2(w3)
    ko = k3 * jnp.exp2(u3)
    st = st_scr[d]

    slabs, cur = [qo], qo
    for dist in range(2, nb):
        cur = cur * _shift_blocks(e_tot, dist - 1, reverse, 1.0)
        slabs.append(cur)
    qs = jnp.concatenate(slabs, axis=0).reshape((nb - 1) * n, HEAD_DIM).astype(BF16)
    sc = _dot_nt(qs, ko.reshape(n, HEAD_DIM).astype(BF16))
    ib = lax.broadcasted_iota(jnp.int32, (n, n), 0) // s
    jb = lax.broadcasted_iota(jnp.int32, (n, n), 1) // s
    bdist = (jb - ib) if reverse else (ib - jb)
    smat = jnp.zeros((n, n), F32)
    for dist in range(1, nb):
        smat = jnp.where(bdist == dist, sc[(dist - 1) * n:dist * n], smat)
    out = _dot(smat.astype(BF16), v16)
    out = out + _dot_nt((qo * jnp.exp2(before3)).reshape(n, HEAD_DIM).astype(BF16), st.astype(BF16))

    prods = [q3 * k3]
    for dd in range(1, s):
        e = jnp.exp2(w3 - _shift_rows(w3, dd, reverse))
        prods.append(jnp.where(same_block(dd), q3 * _shift_rows(k3, dd, reverse) * e, 0.0))
    pst = jnp.concatenate(prods, axis=0).reshape(s * n, HEAD_DIM).astype(BF16)
    sums = _dot(pst, jnp.ones((HEAD_DIM, HEAD_DIM), BF16)).reshape(s * nb, s, HEAD_DIM)
    diag = sums[:nb] * v3
    for dd in range(1, s):
        diag = diag + sums[dd * nb:(dd + 1) * nb] * _shift_rows(v3, dd, reverse)
    o_ref[0, rows, :] = out + diag.reshape(n, HEAD_DIM)

    kst = (ko * jnp.exp2(after3)).reshape(n, HEAD_DIM).astype(BF16)
    st_scr[d] = st * jnp.exp2(total.reshape(1, HEAD_DIM)) + _dot_tn(v16, kst)


def _gla_kernel(qf_ref, zf_ref, vf_ref, qb_ref, zb_ref, vb_ref, lb_ref, s0_ref,
                of_ref, ob_ref, sfin_ref, st_scr, *, tb):
    i = pl.program_id(2)

    @pl.when(i == 0)
    def _():
        st_scr[...] = s0_ref[...]

    lb = lb_ref[...]
    nchunk = tb // HG_CHUNK

    def body(c, carry):
        rf = pl.multiple_of(c * HG_CHUNK, HG_CHUNK)
        _gla_chunk(qf_ref, zf_ref, vf_ref, of_ref, st_scr, 0, rf, lb, False)
        rb = pl.multiple_of((nchunk - 1 - c) * HG_CHUNK, HG_CHUNK)
        _gla_chunk(qb_ref, zb_ref, vb_ref, ob_ref, st_scr, 1, rb, lb, True)
        return carry

    lax.fori_loop(0, nchunk, body, 0, unroll=2)

    @pl.when(i == pl.num_programs(2) - 1)
    def _():
        sfin_ref[...] = st_scr[...]


def _gla_call(p, lb, s0, tb=1024):
    B, T, _ = p.shape
    tb = min(tb, T)
    nb = T // tb
    H = HG_HEADS
    blk = (1, tb, HEAD_DIM)
    fwd = lambda c0: pl.BlockSpec(blk, lambda b, h, i: (b, i, c0 + h))
    bwd = lambda c0: pl.BlockSpec(blk, lambda b, h, i: (b, nb - 1 - i, c0 + h))
    st = pl.BlockSpec((2, None, None, HEAD_DIM, HEAD_DIM), lambda b, h, i: (0, b, h, 0, 0))
    return pl.pallas_call(
        functools.partial(_gla_kernel, tb=tb),
        out_shape=[_sds((B, T, H * HEAD_DIM), F32), _sds((B, T, H * HEAD_DIM), F32),
                   _sds((2, B, H, HEAD_DIM, HEAD_DIM), F32)],
        grid=(B, H, nb),
        in_specs=[fwd(0), fwd(H), fwd(3 * H), bwd(0), bwd(2 * H), bwd(3 * H),
                  pl.BlockSpec((1, HEAD_DIM), lambda b, h, i: (0, h)), st],
        out_specs=[pl.BlockSpec(blk, lambda b, h, i: (b, i, h)),
                   pl.BlockSpec(blk, lambda b, h, i: (b, nb - 1 - i, h)), st],
        scratch_shapes=[pltpu.VMEM((2, HEAD_DIM, HEAD_DIM), F32)],
        compiler_params=_params("parallel", "parallel", "arbitrary"), name="gla",
    )(p, p, p, p, p, p, lb.reshape(1, H * HEAD_DIM), s0)


def _ret_decays(lg, reverse):
    n = RET_CHUNK
    ii = lax.broadcasted_iota(jnp.int32, (n, n), 0)
    jj = lax.broadcasted_iota(jnp.int32, (n, n), 1)
    rel = (jj - ii) if reverse else (ii - jj)
    decay = jnp.where(rel >= 0, jnp.exp(jnp.maximum(rel, 0).astype(F32) * lg), 0.0)
    pos = ((n - 1 - ii) if reverse else ii).astype(F32)
    q_dec = jnp.exp((pos + 1.0) * lg)
    k_dec = jnp.exp((n - 1.0 - pos) * lg)
    c_dec = jnp.exp(n * lg)
    return decay, q_dec, k_dec, c_dec


def _ret_chunk(q_ref, k_ref, v_ref, tabs, o_ref, st_scr, d, r, decays):
    n = RET_CHUNK
    rows = pl.ds(r, n)
    q = q_ref[0, rows, :]
    k = k_ref[0, rows, :]
    if tabs is not None:
        cos, sa, sb = (t[rows, :] for t in tabs)
        q = _rope(q, cos, sa, sb)
        k = _rope(k, cos, sa, sb)
    k = k * (HEAD_DIM ** -0.5)
    v = v_ref[0, rows, :].astype(BF16)
    decay, q_dec, k_dec, c_dec = decays
    st = st_scr[d]
    scores = _dot_nt(q.astype(BF16), k.astype(BF16)) * decay
    out = _dot(scores.astype(BF16), v) + _dot((q * q_dec).astype(BF16), st.astype(BF16))
    o_ref[0, rows, :] = out
    upd = _dot_tn((k * k_dec).astype(BF16), v)
    st_scr[d] = jnp.concatenate([c_dec, c_dec], axis=1) * st + upd


def _ret_kernel(*refs, tb, rotate):
    refs = list(refs)
    qf_ref, kf_ref, vf_ref, qb_ref, kb_ref, vb_ref = refs[:6]
    refs = refs[6:]
    tabs_f = tabs_b = None
    if rotate:
        tabs_f, tabs_b, refs = refs[:3], refs[3:6], refs[6:]
    lg_ref, s0_ref, of_ref, ob_ref, sfin_ref, st_scr = refs
    i = pl.program_id(2)

    @pl.when(i == 0)
    def _():
        st_scr[...] = s0_ref[...]

    nchunk = tb // RET_CHUNK
    dec_f = _ret_decays(lg_ref[0], False)
    dec_b = _ret_decays(lg_ref[1], True)

    def body(c, carry):
        rf = pl.multiple_of(c * RET_CHUNK, RET_CHUNK)
        _ret_chunk(qf_ref, kf_ref, vf_ref, tabs_f, of_ref, st_scr, 0, rf, dec_f)
        rb = pl.multiple_of((nchunk - 1 - c) * RET_CHUNK, RET_CHUNK)
        _ret_chunk(qb_ref, kb_ref, vb_ref, tabs_b, ob_ref, st_scr, 1, rb, dec_b)
        return carry

    lax.fori_loop(0, nchunk, body, 0)

    @pl.when(i == pl.num_programs(2) - 1)
    def _():
        sfin_ref[...] = st_scr[...]


def _ret_call(p, lg, s0, tables, tb=1024):
    B, T, _ = p.shape
    tb = min(tb, T)
    nb = T // tb
    H = RET_HEADS
    rotate = tables is not None
    q0 = (GQA_HEADS + 2 * GQA_KV_HEADS)
    k0 = q0 + H
    v0 = (k0 + H) * HEAD_DIM // RET_DV
    qk = (1, tb, HEAD_DIM)
    vv = (1, tb, RET_DV)
    fwd = lambda shape, c0: pl.BlockSpec(shape, lambda b, h, i: (b, i, c0 + h))
    bwd = lambda shape, c0: pl.BlockSpec(shape, lambda b, h, i: (b, nb - 1 - i, c0 + h))
    in_specs = [fwd(qk, q0), fwd(qk, k0), fwd(vv, v0), bwd(qk, q0), bwd(qk, k0), bwd(vv, v0)]
    args = [p] * 6
    if rotate:
        in_specs += [pl.BlockSpec((tb, HEAD_DIM), lambda b, h, i: (i, 0))] * 3
        in_specs += [pl.BlockSpec((tb, HEAD_DIM), lambda b, h, i: (nb - 1 - i, 0))] * 3
        args += list(tables) * 2
    st = pl.BlockSpec((2, None, None, HEAD_DIM, RET_DV), lambda b, h, i: (0, b, h, 0, 0))
    in_specs += [pl.BlockSpec((2, None, 1, HEAD_DIM), lambda b, h, i: (0, h, 0, 0)), st]
    args += [lg, s0]
    return pl.pallas_call(
        functools.partial(_ret_kernel, tb=tb, rotate=rotate),
        out_shape=[_sds((B, T, H * RET_DV), F32), _sds((B, T, H * RET_DV), F32),
                   _sds((2, B, H, HEAD_DIM, RET_DV), F32)],
        grid=(B, H, nb), in_specs=in_specs,
        out_specs=[pl.BlockSpec(vv, lambda b, h, i: (b, i, h)),
                   pl.BlockSpec(vv, lambda b, h, i: (b, nb - 1 - i, h)), st],
        scratch_shapes=[pltpu.VMEM((2, HEAD_DIM, RET_DV), F32)],
        compiler_params=_params("parallel", "parallel", "arbitrary"), name="ret",
    )(*args)


def _outproj_even_kernel(of_ref, ob_ref, gate_ref, da_ref, x_ref, g5_ref, w_ref, hg_ref, o_ref, a_scr):
    W = HG_HEADS * HEAD_DIM
    for h in range(HG_HEADS):
        sl = slice(h * HEAD_DIM, (h + 1) * HEAD_DIM)
        o = of_ref[0, :, sl] + ob_ref[0, :, sl]
        a_scr[:, sl] = (_rms(o, hg_ref[...]) * _sig(gate_ref[0, :, sl])).astype(BF16)
    y = _dot(a_scr[...], w_ref[:W, :]) + _dot(da_ref[0].astype(BF16), w_ref[W:, :])
    o_ref[0] = x_ref[0] + g5_ref[0] * y


def _outproj_even_call(o_f, o_b, p, da, x, g5, w, hg_g, tm=512):
    B, T, D = x.shape
    tm = min(tm, T)
    W = HG_HEADS * HEAD_DIM
    half = pl.BlockSpec((1, tm, W), lambda b, t: (b, t, 0))
    full = pl.BlockSpec((1, tm, D), lambda b, t: (b, t, 0))
    return pl.pallas_call(
        _outproj_even_kernel, out_shape=_sds((B, T, D), F32), grid=(B, T // tm),
        in_specs=[half, half, pl.BlockSpec((1, tm, W), lambda b, t: (b, t, 4)), half, full,
                  pl.BlockSpec((1, 1, D), lambda b, t: (b, 0, 0)),
                  pl.BlockSpec(w.shape, lambda b, t: (0, 0), pipeline_mode=pl.Buffered(1)),
                  pl.BlockSpec((1, HEAD_DIM), lambda b, t: (0, 0))],
        out_specs=full, scratch_shapes=[pltpu.VMEM((tm, W), BF16)],
        compiler_params=_params("parallel", "parallel"), name="outproj_even",
    )(o_f, o_b, p, da, x, g5, w, hg_g.reshape(1, HEAD_DIM))


def _outproj_odd_kernel(og_ref, of_ref, ob_ref, gate_lo_ref, gate_hi_ref, x_ref, g5_ref, w_ref, rg_ref, o_ref, a_scr):
    W = GQA_HEADS * HEAD_DIM
    half_heads = RET_HEADS // 2
    for h in range(RET_HEADS):
        sl = slice(h * RET_DV, (h + 1) * RET_DV)
        o = of_ref[0, :, sl] + ob_ref[0, :, sl]
        gate_ref = gate_lo_ref if h < half_heads else gate_hi_ref
        gate = gate_ref[0, :, (h % half_heads) * RET_DV:(h % half_heads + 1) * RET_DV]
        a_scr[:, sl] = (_rms(o, rg_ref[...]) * (gate * _sig(gate))).astype(BF16)
    y = _dot(og_ref[0].astype(BF16), w_ref[:W, :]) + _dot(a_scr[...], w_ref[W:, :])
    o_ref[0] = x_ref[0] + g5_ref[0] * y


def _outproj_odd_call(o_g, o_f, o_b, p, x, g5, w, ret_g, tm=512):
    B, T, D = x.shape
    tm = min(tm, T)
    W = RET_HEADS * RET_DV
    gw = W // 2
    gate_blk, rem = divmod(p.shape[2] - W, gw)
    assert rem == 0
    half = pl.BlockSpec((1, tm, W), lambda b, t: (b, t, 0))
    full = pl.BlockSpec((1, tm, D), lambda b, t: (b, t, 0))
    return pl.pallas_call(
        _outproj_odd_kernel, out_shape=_sds((B, T, D), F32), grid=(B, T // tm),
        in_specs=[half, half, half, pl.BlockSpec((1, tm, gw), lambda b, t: (b, t, gate_blk)),
                  pl.BlockSpec((1, tm, gw), lambda b, t: (b, t, gate_blk + 1)), full,
                  pl.BlockSpec((1, 1, D), lambda b, t: (b, 0, 0)),
                  pl.BlockSpec(w.shape, lambda b, t: (0, 0), pipeline_mode=pl.Buffered(1)),
                  pl.BlockSpec((1, RET_DV), lambda b, t: (0, 0))],
        out_specs=full, scratch_shapes=[pltpu.VMEM((tm, W), BF16)],
        compiler_params=_params("parallel", "parallel"), name="outproj_odd",
    )(o_g, o_f, o_b, p, p, x, g5, w, ret_g.reshape(1, RET_DV))


def _rope_tables(n_tokens):
    rows = n_tokens // GRID_W
    row = jnp.repeat(jnp.arange(rows), GRID_W)
    col = jnp.tile(jnp.arange(GRID_W), rows)
    pos = jnp.stack([row, col], axis=-1).astype(F32)
    inv = ROPE_THETA ** (-jnp.arange(ROPE_QUARTER, dtype=F32) / ROPE_QUARTER)
    ang = pos[:, :, None] * inv
    cos, sin = jnp.cos(ang), jnp.sin(ang)
    zero = jnp.zeros_like(sin)
    lanes = lambda a, b: jnp.stack([a, b], axis=2).reshape(n_tokens, HEAD_DIM)
    return lanes(cos, cos), lanes(-sin, zero), lanes(zero, sin)


def _even_mixer(x, ctx, mx, mc, g, w_in, w_out, lb, hg_g, lam_p, da_g, layer_idx, tables, ctx_out):
    B = x.shape[0]
    lam_init = 0.8 - 0.6 * math.exp(-0.3 * layer_idx)
    lam = jnp.exp(jnp.sum(lam_p[0] * lam_p[1])) - jnp.exp(jnp.sum(lam_p[2] * lam_p[3])) + lam_init
    lam_row = jnp.full((1, DA_DV), lam, F32)
    da_gain = da_g.reshape(1, DA_DV)
    px = _inproj_call(x, mx[3], mx[4], g, w_in, tn=1024)
    pc = _inproj_call(ctx, mc[3], mc[4], g, w_in, tn=1024)

    s0 = jnp.zeros((2, B, HG_HEADS, HEAD_DIM, HEAD_DIM), F32)
    ocf, ocb, s_c = _gla_call(pc, lb, s0)
    oxf, oxb, _ = _gla_call(px, lb, s_c)

    qx, kx, vx = _prep_even_call(px, tables)
    qc, kc, vc = _prep_even_call(pc, None)
    attn = functools.partial(_attn_call, n_groups=DA_HEADS, n_q=2, shared_k=False, dv=DA_DV, diff=True,
                             lam=lam_row, gain=da_gain, lam_init=lam_init, tq=512)
    da_x = attn(qx, kx, vx, kc, vc)
    x_new = _outproj_even_call(oxf, oxb, px, da_x, x, mx[5], w_out, hg_g)
    if not ctx_out:
        return x_new, None
    da_c = attn(qc, None, None, kc, vc)
    ctx_new = _outproj_even_call(ocf, ocb, pc, da_c, ctx, mc[5], w_out, hg_g)
    return x_new, ctx_new


def _odd_mixer(x, ctx, mx, mc, g, w_in, w_out, q_g, k_g, decay_logit, ret_g, tables, ctx_out):
    B = x.shape[0]
    px = _inproj_call(x, mx[3], mx[4], g, w_in, tn=w_in.shape[1] // 3)
    pc = _inproj_call(ctx, mc[3], mc[4], g, w_in, tn=w_in.shape[1] // 3)

    qx, kx, vx = _prep_odd_call(px, q_g, k_g, tables)
    qc, kc, vc = _prep_odd_call(pc, q_g, k_g, None)
    attn = functools.partial(_attn_call, n_groups=GQA_KV_HEADS, n_q=GQA_HEADS // GQA_KV_HEADS, shared_k=True,
                             dv=HEAD_DIM)
    og_x = attn(qx, kx, vx, kc, vc)

    log_gamma = jax.nn.log_sigmoid(decay_logit.astype(F32))
    lg = jnp.broadcast_to(log_gamma[:, :, None, None], (2, RET_HEADS, 1, HEAD_DIM))
    s0 = jnp.zeros((2, B, RET_HEADS, HEAD_DIM, RET_DV), F32)
    ocf, ocb, s_c = _ret_call(pc, lg, s0, None)
    oxf, oxb, _ = _ret_call(px, lg, s_c, tables)
    x_new = _outproj_odd_call(og_x, oxf, oxb, px, x, mx[5], w_out, ret_g)
    if not ctx_out:
        return x_new, None
    og_c = attn(qc, None, None, kc, vc)
    ctx_new = _outproj_odd_call(og_c, ocf, ocb, pc, ctx, mc[5], w_out, ret_g)
    return x_new, ctx_new


def kernel(x, c, ctx, c_ctx, mod_w, mod_b, norm_g, ffn_w13, ffn_w2, even_w_in, even_w_out, hgrn_lb_logits, hgrn_norm_g, diff_lambda, diff_norm_g, odd_w_in, odd_w_out, gqa_q_norm_g, gqa_k_norm_g, ret_decay_logit, ret_norm_g, final_norm_g):
    B, T, D = x.shape
    C = ctx.shape[1]
    depth = mod_w.shape[0]
    tables = _rope_tables(T)
    lower_bounds = jnp.cumsum(jax.nn.softmax(hgrn_lb_logits.astype(F32), axis=0), axis=0)

    rows = -(-(B + 1) // 8) * 8
    cc = jnp.zeros((rows, D), F32).at[:B].set(c).at[B].set(c_ctx)
    mod = _mod_call(cc, mod_w, mod_b)
    mod_x = mod[:, :B].reshape(depth, B, N_MOD, D).transpose(0, 2, 1, 3)[:, :, :, None, :]
    mod_c = mod[:, B].reshape(depth, N_MOD, 1, 1, D)

    w13 = ffn_w13.astype(BF16)
    w2 = ffn_w2.astype(BF16)
    even_in, even_out = even_w_in.astype(BF16), even_w_out.astype(BF16)
    odd_in, odd_out = odd_w_in.astype(BF16), odd_w_out.astype(BF16)

    ctx = ctx.reshape(1, B * C, D)
    for l in range(depth):
        last = l == depth - 1
        mx, mc = mod_x[l], mod_c[l]
        g = norm_g[l]
        x = _ffn_call(x, mx[0], mx[1], mx[2], g[0], w13, w2, l, 0)
        ctx = _ffn_call(ctx, mc[0], mc[1], mc[2], g[0], w13, w2, l, 0)
        ctx_b = ctx.reshape(B, C, D)
        mc_b = jnp.broadcast_to(mc, (N_MOD, B, 1, D))
        if l % 2 == 0:
            e = l // 2
            x, ctx_b = _even_mixer(x, ctx_b, mx, mc_b, g[1], even_in[e], even_out[e], lower_bounds[l], hgrn_norm_g[e],
                                   diff_lambda[e], diff_norm_g[e], l, tables, not last)
        else:
            o = l // 2
            x, ctx_b = _odd_mixer(x, ctx_b, mx, mc_b, g[1], odd_in[o], odd_out[o], gqa_q_norm_g[o], gqa_k_norm_g[o],
                                  ret_decay_logit[o], ret_norm_g[o], tables, not last)
        x = _ffn_call(x, mx[6], mx[7], mx[8], g[2], w13, w2, l, 1, final_g=final_norm_g if last else None)
        if not last:
            ctx = _ffn_call(ctx_b.reshape(1, B * C, D), mc[6], mc[7], mc[8], g[2], w13, w2, l, 1)
    return x
```

```python
import functools
import math

import jax
import jax.numpy as jnp
from jax import lax
from jax.experimental import pallas as pl
from jax.experimental.pallas import tpu as pltpu

F32 = jnp.float32
BF16 = jnp.bfloat16

NORM_EPS = 1e-6
GRID_W = 64
HEAD_DIM = 128
ROPE_QUARTER = HEAD_DIM // 4
ROPE_THETA = 10000.0
N_MOD = 9

HG_HEADS = 8
HG_CHUNK = 64
HG_SUB = 8
DA_HEADS = 4
DA_DV = 256
GQA_HEADS = 8
GQA_KV_HEADS = 2
RET_HEADS = 4
RET_DV = 256
RET_CHUNK = 128
FFN_TF = 512
FFN_VMEM_LIMIT = 62 * 1024 * 1024
ROW_SLAB = 256
LOG2E = 1.4426950408889634

VMEM_LIMIT = 56 * 1024 * 1024


def _sds(shape, dtype):
    return jax.ShapeDtypeStruct(shape, dtype)


def _params(*sem, vmem=VMEM_LIMIT):
    return pltpu.CompilerParams(dimension_semantics=sem, vmem_limit_bytes=vmem)


def _sig(x):
    return 1.0 / (1.0 + jnp.exp(-x))


def _rms(x, g):
    return x * lax.rsqrt(jnp.mean(x * x, axis=-1, keepdims=True) + NORM_EPS) * g


def _dot(a, b):
    return jnp.dot(a, b, preferred_element_type=F32)


def _dot_nt(a, b):
    return lax.dot_general(a, b, (((1,), (1,)), ((), ())), preferred_element_type=F32)


def _dot_tn(a, b):
    return lax.dot_general(a, b, (((0,), (0,)), ((), ())), preferred_element_type=F32)


def _rope(x, cos, sa, sb):
    return x * cos + pltpu.roll(x, 96, 1) * sa + pltpu.roll(x, 32, 1) * sb


def _modulate_into(h_scr, x_ref, g_ref, sc_ref, sh_ref):
    gain = g_ref[...] * (1.0 + sc_ref[0])
    shift = sh_ref[0]
    for r in range(0, x_ref.shape[1], ROW_SLAB):
        rows = slice(r, r + ROW_SLAB)
        x = x_ref[0, rows, :]
        inv = lax.rsqrt(jnp.mean(x * x, axis=-1, keepdims=True) + NORM_EPS)
        h_scr[rows, :] = (x * inv * gain + shift).astype(BF16)


def _mod_kernel(c_ref, w_ref, b_ref, o_ref):
    c = c_ref[...]
    a = (c * _sig(c)).astype(BF16)
    o_ref[0] = _dot(a, w_ref[0].astype(BF16)) + b_ref[0]


def _mod_call(cc, mod_w, mod_b):
    L, D, N = mod_w.shape
    R = cc.shape[0]
    tn = 1024
    return pl.pallas_call(
        _mod_kernel, out_shape=_sds((L, R, N), F32), grid=(L, N // tn),
        in_specs=[pl.BlockSpec((R, D), lambda l, j: (0, 0)),
                  pl.BlockSpec((1, D, tn), lambda l, j: (l, 0, j)),
                  pl.BlockSpec((1, 1, tn), lambda l, j: (l, 0, j))],
        out_specs=pl.BlockSpec((1, R, tn), lambda l, j: (l, 0, j)),
        compiler_params=_params("parallel", "parallel"), name="mod",
    )(cc, mod_w, mod_b.reshape(L, 1, N))


def _ffn_kernel(*refs, final):
    if final:
        x_ref, sh_ref, sc_ref, gt_ref, g_ref, w1_ref, w3_ref, w2_ref, fg_ref, o_ref, h_scr = refs
    else:
        x_ref, sh_ref, sc_ref, gt_ref, g_ref, w1_ref, w3_ref, w2_ref, o_ref, h_scr = refs
    j = pl.program_id(2)

    @pl.when(j == 0)
    def _():
        _modulate_into(h_scr, x_ref, g_ref, sc_ref, sh_ref)
        o_ref[...] = jnp.zeros_like(o_ref)

    h = h_scr[...]
    a = _dot(h, w1_ref[...])
    b = _dot(h, w3_ref[...])
    u = (a * _sig(a) * b).astype(BF16)
    o_ref[0] += _dot(u, w2_ref[...])

    @pl.when(j == pl.num_programs(2) - 1)
    def _():
        for r in range(0, x_ref.shape[1], ROW_SLAB):
            rows = slice(r, r + ROW_SLAB)
            out = x_ref[0, rows, :] + 0.5 * gt_ref[0] * o_ref[0, rows, :]
            if final:
                out = _rms(out, fg_ref[...])
            o_ref[0, rows, :] = out


def _ffn_call(x, sh, sc, gt, g, w13, w2, l, i, final_g=None, tm=1024, tf=FFN_TF):
    B, T, D = x.shape
    nf = w2.shape[2] // tf
    tm = min(tm, T)
    vec = pl.BlockSpec((1, 1, D), lambda b, t, j: (b, 0, 0))
    row = pl.BlockSpec((1, D), lambda b, t, j: (0, 0))
    in_specs = [pl.BlockSpec((1, tm, D), lambda b, t, j: (b, t, 0)), vec, vec, vec, row,
                pl.BlockSpec((None, None, D, tf), lambda b, t, j: (l, i, 0, j)),
                pl.BlockSpec((None, None, D, tf), lambda b, t, j: (l, i, 0, nf + j)),
                pl.BlockSpec((None, None, tf, D), lambda b, t, j: (l, i, j, 0))]
    args = [x, sh, sc, gt, g.reshape(1, D), w13, w13, w2]
    if final_g is not None:
        in_specs.append(row)
        args.append(final_g.reshape(1, D))
    return pl.pallas_call(
        functools.partial(_ffn_kernel, final=final_g is not None),
        out_shape=_sds((B, T, D), F32), grid=(B, T // tm, nf),
        in_specs=in_specs, out_specs=pl.BlockSpec((1, tm, D), lambda b, t, j: (b, t, 0)),
        scratch_shapes=[pltpu.VMEM((tm, D), BF16)],
        compiler_params=_params("parallel", "parallel", "arbitrary", vmem=FFN_VMEM_LIMIT), name="ffn",
    )(*args)


def _inproj_kernel(x_ref, sh_ref, sc_ref, g_ref, w_ref, o_ref, h_scr):
    @pl.when(pl.program_id(2) == 0)
    def _():
        _modulate_into(h_scr, x_ref, g_ref, sc_ref, sh_ref)

    o_ref[0] = _dot(h_scr[...], w_ref[...])


def _inproj_call(x, sh, sc, g, w, tn, tm=1024):
    B, T, D = x.shape
    N = w.shape[1]
    tm = min(tm, T)
    vec = pl.BlockSpec((1, 1, D), lambda b, t, j: (b, 0, 0))
    return pl.pallas_call(
        _inproj_kernel, out_shape=_sds((B, T, N), F32), grid=(B, T // tm, N // tn),
        in_specs=[pl.BlockSpec((1, tm, D), lambda b, t, j: (b, t, 0)), vec, vec,
                  pl.BlockSpec((1, D), lambda b, t, j: (0, 0)),
                  pl.BlockSpec((D, tn), lambda b, t, j: (0, j))],
        out_specs=pl.BlockSpec((1, tm, tn), lambda b, t, j: (b, t, j)),
        scratch_shapes=[pltpu.VMEM((tm, D), BF16)],
        compiler_params=_params("parallel", "parallel", "arbitrary"), name="inproj",
    )(x, sh, sc, g.reshape(1, D), w)


def _prep_even_kernel(*refs, rotate):
    if rotate:
        q_ref, k_ref, v_ref, cos_ref, sa_ref, sb_ref, qo_ref, ko_ref, vo_ref = refs
        cos, sa, sb = cos_ref[...], sa_ref[...], sb_ref[...]
    else:
        q_ref, k_ref, v_ref, qo_ref, ko_ref, vo_ref = refs
    scale = HEAD_DIM ** -0.5 * LOG2E
    for h in range(2 * DA_HEADS):
        sl = slice(h * HEAD_DIM, (h + 1) * HEAD_DIM)
        q = q_ref[0, :, sl]
        k = k_ref[0, :, sl]
        if rotate:
            q = _rope(q, cos, sa, sb)
            k = _rope(k, cos, sa, sb)
        qo_ref[0, :, sl] = (q * scale).astype(BF16)
        ko_ref[0, :, sl] = k.astype(BF16)
    vo_ref[0] = v_ref[0].astype(BF16)


def _prep_even_call(p, tables, tp=1024):
    B, T, _ = p.shape
    tp = min(tp, T)
    W = 2 * DA_HEADS * HEAD_DIM
    rotate = tables is not None
    in_specs = [pl.BlockSpec((1, tp, W), lambda b, t: (b, t, 5)),
                pl.BlockSpec((1, tp, W), lambda b, t: (b, t, 6)),
                pl.BlockSpec((1, tp, W), lambda b, t: (b, t, 7))]
    args = [p, p, p]
    if rotate:
        in_specs += [pl.BlockSpec((tp, HEAD_DIM), lambda b, t: (t, 0))] * 3
        args += list(tables)
    out = pl.BlockSpec((1, tp, W), lambda b, t: (b, t, 0))
    return pl.pallas_call(
        functools.partial(_prep_even_kernel, rotate=rotate),
        out_shape=[_sds((B, T, W), BF16)] * 3, grid=(B, T // tp),
        in_specs=in_specs, out_specs=[out, out, out],
        compiler_params=_params("parallel", "parallel"), name="prep_even",
    )(*args)


def _prep_odd_kernel(*refs, rotate):
    if rotate:
        q_ref, k_ref, v_ref, qg_ref, kg_ref, cos_ref, sa_ref, sb_ref, qo_ref, ko_ref, vo_ref = refs
        cos, sa, sb = cos_ref[...], sa_ref[...], sb_ref[...]
    else:
        q_ref, k_ref, v_ref, qg_ref, kg_ref, qo_ref, ko_ref, vo_ref = refs
    scale = HEAD_DIM ** -0.5 * LOG2E
    for h in range(GQA_HEADS):
        sl = slice(h * HEAD_DIM, (h + 1) * HEAD_DIM)
        q = _rms(q_ref[0, :, sl], qg_ref[...])
        if rotate:
            q = _rope(q, cos, sa, sb)
        qo_ref[0, :, sl] = (q * scale).astype(BF16)
    for h in range(GQA_KV_HEADS):
        sl = slice(h * HEAD_DIM, (h + 1) * HEAD_DIM)
        k = _rms(k_ref[0, :, sl], kg_ref[...])
        if rotate:
            k = _rope(k, cos, sa, sb)
        ko_ref[0, :, sl] = k.astype(BF16)
    vo_ref[0] = v_ref[0].astype(BF16)


def _prep_odd_call(p, q_g, k_g, tables, tp=1024):
    B, T, _ = p.shape
    tp = min(tp, T)
    QW = GQA_HEADS * HEAD_DIM
    KW = GQA_KV_HEADS * HEAD_DIM
    rotate = tables is not None
    gain = pl.BlockSpec((1, HEAD_DIM), lambda b, t: (0, 0))
    in_specs = [pl.BlockSpec((1, tp, QW), lambda b, t: (b, t, 0)),
                pl.BlockSpec((1, tp, KW), lambda b, t: (b, t, QW // KW)),
                pl.BlockSpec((1, tp, KW), lambda b, t: (b, t, QW // KW + 1)), gain, gain]
    args = [p, p, p, q_g.reshape(1, HEAD_DIM), k_g.reshape(1, HEAD_DIM)]
    if rotate:
        in_specs += [pl.BlockSpec((tp, HEAD_DIM), lambda b, t: (t, 0))] * 3
        args += list(tables)
    return pl.pallas_call(
        functools.partial(_prep_odd_kernel, rotate=rotate),
        out_shape=[_sds((B, T, QW), BF16), _sds((B, T, KW), BF16), _sds((B, T, KW), BF16)],
        grid=(B, T // tp), in_specs=in_specs,
        out_specs=[pl.BlockSpec((1, tp, QW), lambda b, t: (b, t, 0)),
                   pl.BlockSpec((1, tp, KW), lambda b, t: (b, t, 0)),
                   pl.BlockSpec((1, tp, KW), lambda b, t: (b, t, 0))],
        compiler_params=_params("parallel", "parallel"), name="prep_odd",
    )(*args)


def _attn_kernel(*refs, n_q, shared_k, has_lat, diff, lam_init, kv_chunk):
    refs = list(refs)
    q_ref = refs.pop(0)
    if has_lat:
        kl_ref, vl_ref = refs.pop(0), refs.pop(0)
    kc_ref, vc_ref = refs.pop(0), refs.pop(0)
    if diff:
        lam_ref, g_ref = refs.pop(0), refs.pop(0)
    o_ref = refs.pop(0)
    tq = q_ref.shape[1]
    heads = [q_ref[0, :, h * HEAD_DIM:(h + 1) * HEAD_DIM] for h in range(n_q)]
    if shared_k:
        heads = [jnp.concatenate(heads, axis=0)]
    pieces = [(kc_ref, vc_ref, 0, kc_ref.shape[1])]
    if has_lat:
        pieces += [(kl_ref, vl_ref, c * kv_chunk, kv_chunk) for c in range(kl_ref.shape[1] // kv_chunk)]
    state = [None] * len(heads)
    for k_ref, v_ref, start, size in pieces:
        rows = slice(start, start + size)
        for h, qh in enumerate(heads):
            kh = 0 if shared_k else h
            s = _dot_nt(qh, k_ref[0, rows, kh * HEAD_DIM:(kh + 1) * HEAD_DIM])
            m_new = jnp.max(s, axis=-1, keepdims=True)
            if state[h] is not None:
                m, den, acc = state[h]
                m_new = jnp.maximum(m, m_new)
            p = jnp.exp2(s - m_new)
            part = p[:, :HEAD_DIM]
            for c in range(1, size // HEAD_DIM):
                part = part + p[:, c * HEAD_DIM:(c + 1) * HEAD_DIM]
            pv = _dot(p.astype(BF16), v_ref[0, rows, :])
            if state[h] is not None:
                a = jnp.exp2(m - m_new)
                part = a * den + part
                pv = a * acc + pv
            state[h] = (m_new, part, pv)
    outs = [acc / jnp.sum(den, axis=-1, keepdims=True) for _, den, acc in state]
    if diff:
        d = outs[0] - lam_ref[...] * outs[1]
        o_ref[0] = _rms(d, g_ref[...]) * (1.0 - lam_init)
    else:
        for h in range(n_q):
            o_ref[0, :, h * HEAD_DIM:(h + 1) * HEAD_DIM] = outs[0][h * tq:(h + 1) * tq]


def _attn_call(q, k_lat, v_lat, k_ctx, v_ctx, *, n_groups, n_q, shared_k, dv, diff=False, lam=None, gain=None,
               lam_init=0.0, tq=256, kv_chunk=1024):
    assert (diff and n_q == 2 and not shared_k) or (shared_k and not diff)
    B, Tq, _ = q.shape
    C = k_ctx.shape[1]
    tq = min(tq, Tq)
    has_lat = k_lat is not None
    qw = n_q * HEAD_DIM
    kw = HEAD_DIM if shared_k else qw
    in_specs = [pl.BlockSpec((1, tq, qw), lambda b, g, t: (b, t, g))]
    args = [q]
    if has_lat:
        T = k_lat.shape[1]
        in_specs += [pl.BlockSpec((1, T, kw), lambda b, g, t: (b, 0, g)),
                     pl.BlockSpec((1, T, dv), lambda b, g, t: (b, 0, g))]
        args += [k_lat, v_lat]
    in_specs += [pl.BlockSpec((1, C, kw), lambda b, g, t: (b, 0, g)),
                 pl.BlockSpec((1, C, dv), lambda b, g, t: (b, 0, g))]
    args += [k_ctx, v_ctx]
    if diff:
        in_specs += [pl.BlockSpec((1, dv), lambda b, g, t: (0, 0))] * 2
        args += [lam, gain]
    ow = dv if diff else qw
    return pl.pallas_call(
        functools.partial(_attn_kernel, n_q=n_q, shared_k=shared_k, has_lat=has_lat, diff=diff, lam_init=lam_init,
                          kv_chunk=min(kv_chunk, k_lat.shape[1]) if has_lat else 0),
        out_shape=_sds((B, Tq, n_groups * ow), F32), grid=(B, n_groups, Tq // tq),
        in_specs=in_specs, out_specs=pl.BlockSpec((1, tq, ow), lambda b, g, t: (b, t, g)),
        compiler_params=_params("parallel", "parallel", "arbitrary"), name="attn",
    )(*args)


def _shift_rows(x3, d, reverse):
    if d == 0:
        return x3
    s = x3.shape[1]
    return pltpu.roll(x3, (s - d) if reverse else d, 1)


def _shift_blocks(x3, d, reverse, fill):
    pad = jnp.full((d,) + x3.shape[1:], fill, x3.dtype)
    if reverse:
        return jnp.concatenate([x3[d:], pad], axis=0)
    return jnp.concatenate([pad, x3[:x3.shape[0] - d]], axis=0)


def _gla_chunk(q_ref, z_ref, v_ref, o_ref, st_scr, d, r, lb, reverse):
    n, s = HG_CHUNK, HG_SUB
    nb = n // s
    rows = pl.ds(r, n)
    shape3 = (nb, s, HEAD_DIM)
    qraw = q_ref[0, rows, :]
    z = z_ref[0, rows, :]
    v = v_ref[0, rows, :]
    q3 = (qraw * _sig(qraw) * (HEAD_DIM ** -0.5)).reshape(shape3)
    sz = _sig(z)
    k3 = ((1.0 - lb) * (1.0 - sz)).reshape(shape3)
    g3 = jnp.log2(lb + (1.0 - lb) * sz).reshape(shape3)
    v3 = v.reshape(shape3)
    v16 = v.astype(BF16)
    row = lax.broadcasted_iota(jnp.int32, shape3, 1)

    def same_block(dd):
        return (row < s - dd) if reverse else (row >= dd)

    w3 = g3
    for sh in (1, 2, 4):
        w3 = w3 + jnp.where(same_block(sh), _shift_rows(w3, sh, reverse), 0.0)
    tot = w3[:, :1, :] if reverse else w3[:, s - 1:, :]
    u3 = jnp.minimum(tot - w3, 0.0)
    order = range(nb - 1, -1, -1) if reverse else range(nb)
    before, after = [None] * nb, [None] * nb
    acc = jnp.zeros((1, 1, HEAD_DIM), F32)
    for blk in order:
        before[blk] = acc
        acc = acc + tot[blk:blk + 1]
    total = acc
    acc = jnp.zeros((1, 1, HEAD_DIM), F32)
    for blk in reversed(order):
        after[blk] = acc
        acc = acc + tot[blk:blk + 1]
    before3 = jnp.concatenate(before, axis=0)
    after3 = jnp.concatenate(after, axis=0)

    e_tot = jnp.exp2(tot)
    qo = q3 * jnp.exp2(w3)
    ko = k3 * jnp.exp2(u3)
    st = st_scr[d]

    slabs, cur = [qo], qo
    for dist in range(2, nb):
        cur = cur * _shift_blocks(e_tot, dist - 1, reverse, 1.0)
        slabs.append(cur)
    qs = jnp.concatenate(slabs, axis=0).reshape((nb - 1) * n, HEAD_DIM).astype(BF16)
    sc = _dot_nt(qs, ko.reshape(n, HEAD_DIM).astype(BF16))
    ib = lax.broadcasted_iota(jnp.int32, (n, n), 0) // s
    jb = lax.broadcasted_iota(jnp.int32, (n, n), 1) // s
    bdist = (jb - ib) if reverse else (ib - jb)
    smat = jnp.zeros((n, n), F32)
    for dist in range(1, nb):
        smat = jnp.where(bdist == dist, sc[(dist - 1) * n:dist * n], smat)
    out = _dot(smat.astype(BF16), v16)
    out = out + _dot_nt((qo * jnp.exp2(before3)).reshape(n, HEAD_DIM).astype(BF16), st.astype(BF16))

    prods = [q3 * k3]
    for dd in range(1, s):
        e = jnp.exp2(w3 - _shift_rows(w3, dd, reverse))
        prods.append(jnp.where(same_block(dd), q3 * _shift_rows(k3, dd, reverse) * e, 0.0))
    pst = jnp.concatenate(prods, axis=0).reshape(s * n, HEAD_DIM).astype(BF16)
    sums = _dot(pst, jnp.ones((HEAD_DIM, HEAD_DIM), BF16)).reshape(s * nb, s, HEAD_DIM)
    diag = sums[:nb] * v3
    for dd in range(1, s):
        diag = diag + sums[dd * nb:(dd + 1) * nb] * _shift_rows(v3, dd, reverse)
    o_ref[0, rows, :] = out + diag.reshape(n, HEAD_DIM)

    kst = (ko * jnp.exp2(after3)).reshape(n, HEAD_DIM).astype(BF16)
    st_scr[d] = st * jnp.exp2(total.reshape(1, HEAD_DIM)) + _dot_tn(v16, kst)


def _gla_kernel(qf_ref, zf_ref, vf_ref, qb_ref, zb_ref, vb_ref, lb_ref, s0_ref,
                of_ref, ob_ref, sfin_ref, st_scr, *, tb):
    i = pl.program_id(2)

    @pl.when(i == 0)
    def _():
        st_scr[...] = s0_ref[...]

    lb = lb_ref[...]
    nchunk = tb // HG_CHUNK

    def body(c, carry):
        rf = pl.multiple_of(c * HG_CHUNK, HG_CHUNK)
        _gla_chunk(qf_ref, zf_ref, vf_ref, of_ref, st_scr, 0, rf, lb, False)
        rb = pl.multiple_of((nchunk - 1 - c) * HG_CHUNK, HG_CHUNK)
        _gla_chunk(qb_ref, zb_ref, vb_ref, ob_ref, st_scr, 1, rb, lb, True)
        return carry

    lax.fori_loop(0, nchunk, body, 0, unroll=4)

    @pl.when(i == pl.num_programs(2) - 1)
    def _():
        sfin_ref[...] = st_scr[...]


def _gla_call(p, lb, s0, tb=4096):
    B, T, _ = p.shape
    tb = min(tb, T)
    nb = T // tb
    H = HG_HEADS
    blk = (1, tb, HEAD_DIM)
    fwd = lambda c0: pl.BlockSpec(blk, lambda b, h, i: (b, i, c0 + h))
    bwd = lambda c0: pl.BlockSpec(blk, lambda b, h, i: (b, nb - 1 - i, c0 + h))
    st = pl.BlockSpec((2, None, None, HEAD_DIM, HEAD_DIM), lambda b, h, i: (0, b, h, 0, 0))
    return pl.pallas_call(
        functools.partial(_gla_kernel, tb=tb),
        out_shape=[_sds((B, T, H * HEAD_DIM), F32), _sds((B, T, H * HEAD_DIM), F32),
                   _sds((2, B, H, HEAD_DIM, HEAD_DIM), F32)],
        grid=(B, H, nb),
        in_specs=[fwd(0), fwd(H), fwd(3 * H), bwd(0), bwd(2 * H), bwd(3 * H),
                  pl.BlockSpec((1, HEAD_DIM), lambda b, h, i: (0, h)), st],
        out_specs=[pl.BlockSpec(blk, lambda b, h, i: (b, i, h)),
                   pl.BlockSpec(blk, lambda b, h, i: (b, nb - 1 - i, h)), st],
        scratch_shapes=[pltpu.VMEM((2, HEAD_DIM, HEAD_DIM), F32)],
        compiler_params=_params("parallel", "parallel", "arbitrary"), name="gla",
    )(p, p, p, p, p, p, lb.reshape(1, H * HEAD_DIM), s0)


def _ret_decays(lg, reverse):
    n = RET_CHUNK
    ii = lax.broadcasted_iota(jnp.int32, (n, n), 0)
    jj = lax.broadcasted_iota(jnp.int32, (n, n), 1)
    rel = (jj - ii) if reverse else (ii - jj)
    decay = jnp.where(rel >= 0, jnp.exp(jnp.maximum(rel, 0).astype(F32) * lg), 0.0)
    pos = ((n - 1 - ii) if reverse else ii).astype(F32)
    q_dec = jnp.exp((pos + 1.0) * lg)
    k_dec = jnp.exp((n - 1.0 - pos) * lg)
    c_dec = jnp.exp(n * lg)
    return decay, q_dec, k_dec, c_dec


def _ret_chunk(q_ref, k_ref, v_ref, tabs, o_ref, st_scr, d, r, decays):
    n = RET_CHUNK
    rows = pl.ds(r, n)
    q = q_ref[0, rows, :]
    k = k_ref[0, rows, :]
    if tabs is not None:
        cos, sa, sb = (t[rows, :] for t in tabs)
        q = _rope(q, cos, sa, sb)
        k = _rope(k, cos, sa, sb)
    k = k * (HEAD_DIM ** -0.5)
    v = v_ref[0, rows, :].astype(BF16)
    decay, q_dec, k_dec, c_dec = decays
    st = st_scr[d]
    scores = _dot_nt(q.astype(BF16), k.astype(BF16)) * decay
    out = _dot(scores.astype(BF16), v) + _dot((q * q_dec).astype(BF16), st.astype(BF16))
    o_ref[0, rows, :] = out
    upd = _dot_tn((k * k_dec).astype(BF16), v)
    st_scr[d] = jnp.concatenate([c_dec, c_dec], axis=1) * st + upd


def _ret_kernel(*refs, tb, rotate):
    refs = list(refs)
    qf_ref, kf_ref, vf_ref, qb_ref, kb_ref, vb_ref = refs[:6]
    refs = refs[6:]
    tabs_f = tabs_b = None
    if rotate:
        tabs_f, tabs_b, refs = refs[:3], refs[3:6], refs[6:]
    lg_ref, s0_ref, of_ref, ob_ref, sfin_ref, st_scr = refs
    i = pl.program_id(2)

    @pl.when(i == 0)
    def _():
        st_scr[...] = s0_ref[...]

    nchunk = tb // RET_CHUNK
    dec_f = _ret_decays(lg_ref[0], False)
    dec_b = _ret_decays(lg_ref[1], True)

    def body(c, carry):
        rf = pl.multiple_of(c * RET_CHUNK, RET_CHUNK)
        _ret_chunk(qf_ref, kf_ref, vf_ref, tabs_f, of_ref, st_scr, 0, rf, dec_f)
        rb = pl.multiple_of((nchunk - 1 - c) * RET_CHUNK, RET_CHUNK)
        _ret_chunk(qb_ref, kb_ref, vb_ref, tabs_b, ob_ref, st_scr, 1, rb, dec_b)
        return carry

    lax.fori_loop(0, nchunk, body, 0)

    @pl.when(i == pl.num_programs(2) - 1)
    def _():
        sfin_ref[...] = st_scr[...]


def _ret_call(p, lg, s0, tables, tb=2048):
    B, T, _ = p.shape
    tb = min(tb, T)
    nb = T // tb
    H = RET_HEADS
    rotate = tables is not None
    q0 = (GQA_HEADS + 2 * GQA_KV_HEADS)
    k0 = q0 + H
    v0 = (k0 + H) * HEAD_DIM // RET_DV
    qk = (1, tb, HEAD_DIM)
    vv = (1, tb, RET_DV)
    fwd = lambda shape, c0: pl.BlockSpec(shape, lambda b, h, i: (b, i, c0 + h))
    bwd = lambda shape, c0: pl.BlockSpec(shape, lambda b, h, i: (b, nb - 1 - i, c0 + h))
    in_specs = [fwd(qk, q0), fwd(qk, k0), fwd(vv, v0), bwd(qk, q0), bwd(qk, k0), bwd(vv, v0)]
    args = [p] * 6
    if rotate:
        in_specs += [pl.BlockSpec((tb, HEAD_DIM), lambda b, h, i: (i, 0))] * 3
        in_specs += [pl.BlockSpec((tb, HEAD_DIM), lambda b, h, i: (nb - 1 - i, 0))] * 3
        args += list(tables) * 2
    st = pl.BlockSpec((2, None, None, HEAD_DIM, RET_DV), lambda b, h, i: (0, b, h, 0, 0))
    in_specs += [pl.BlockSpec((2, None, 1, HEAD_DIM), lambda b, h, i: (0, h, 0, 0)), st]
    args += [lg, s0]
    return pl.pallas_call(
        functools.partial(_ret_kernel, tb=tb, rotate=rotate),
        out_shape=[_sds((B, T, H * RET_DV), F32), _sds((B, T, H * RET_DV), F32),
                   _sds((2, B, H, HEAD_DIM, RET_DV), F32)],
        grid=(B, H, nb), in_specs=in_specs,
        out_specs=[pl.BlockSpec(vv, lambda b, h, i: (b, i, h)),
                   pl.BlockSpec(vv, lambda b, h, i: (b, nb - 1 - i, h)), st],
        scratch_shapes=[pltpu.VMEM((2, HEAD_DIM, RET_DV), F32)],
        compiler_params=_params("parallel", "parallel", "arbitrary"), name="ret",
    )(*args)


def _outproj_even_kernel(of_ref, ob_ref, gate_ref, da_ref, x_ref, g5_ref, w_ref, hg_ref, o_ref, a_scr):
    W = HG_HEADS * HEAD_DIM
    for h in range(HG_HEADS):
        sl = slice(h * HEAD_DIM, (h + 1) * HEAD_DIM)
        o = of_ref[0, :, sl] + ob_ref[0, :, sl]
        a_scr[:, sl] = (_rms(o, hg_ref[...]) * _sig(gate_ref[0, :, sl])).astype(BF16)
    y = _dot(a_scr[...], w_ref[:W, :]) + _dot(da_ref[0].astype(BF16), w_ref[W:, :])
    o_ref[0] = x_ref[0] + g5_ref[0] * y


def _outproj_even_call(o_f, o_b, p, da, x, g5, w, hg_g, tm=512):
    B, T, D = x.shape
    tm = min(tm, T)
    W = HG_HEADS * HEAD_DIM
    half = pl.BlockSpec((1, tm, W), lambda b, t: (b, t, 0))
    full = pl.BlockSpec((1, tm, D), lambda b, t: (b, t, 0))
    return pl.pallas_call(
        _outproj_even_kernel, out_shape=_sds((B, T, D), F32), grid=(B, T // tm),
        in_specs=[half, half, pl.BlockSpec((1, tm, W), lambda b, t: (b, t, 4)), half, full,
                  pl.BlockSpec((1, 1, D), lambda b, t: (b, 0, 0)),
                  pl.BlockSpec(w.shape, lambda b, t: (0, 0), pipeline_mode=pl.Buffered(1)),
                  pl.BlockSpec((1, HEAD_DIM), lambda b, t: (0, 0))],
        out_specs=full, scratch_shapes=[pltpu.VMEM((tm, W), BF16)],
        compiler_params=_params("parallel", "parallel"), name="outproj_even",
    )(o_f, o_b, p, da, x, g5, w, hg_g.reshape(1, HEAD_DIM))


def _outproj_odd_kernel(og_ref, of_ref, ob_ref, gate_lo_ref, gate_hi_ref, x_ref, g5_ref, w_ref, rg_ref, o_ref, a_scr):
    W = GQA_HEADS * HEAD_DIM
    half_heads = RET_HEADS // 2
    for h in range(RET_HEADS):
        sl = slice(h * RET_DV, (h + 1) * RET_DV)
        o = of_ref[0, :, sl] + ob_ref[0, :, sl]
        gate_ref = gate_lo_ref if h < half_heads else gate_hi_ref
        gate = gate_ref[0, :, (h % half_heads) * RET_DV:(h % half_heads + 1) * RET_DV]
        a_scr[:, sl] = (_rms(o, rg_ref[...]) * (gate * _sig(gate))).astype(BF16)
    y = _dot(og_ref[0].astype(BF16), w_ref[:W, :]) + _dot(a_scr[...], w_ref[W:, :])
    o_ref[0] = x_ref[0] + g5_ref[0] * y


def _outproj_odd_call(o_g, o_f, o_b, p, x, g5, w, ret_g, tm=512):
    B, T, D = x.shape
    tm = min(tm, T)
    W = RET_HEADS * RET_DV
    gw = W // 2
    gate_blk, rem = divmod(p.shape[2] - W, gw)
    assert rem == 0
    half = pl.BlockSpec((1, tm, W), lambda b, t: (b, t, 0))
    full = pl.BlockSpec((1, tm, D), lambda b, t: (b, t, 0))
    return pl.pallas_call(
        _outproj_odd_kernel, out_shape=_sds((B, T, D), F32), grid=(B, T // tm),
        in_specs=[half, half, half, pl.BlockSpec((1, tm, gw), lambda b, t: (b, t, gate_blk)),
                  pl.BlockSpec((1, tm, gw), lambda b, t: (b, t, gate_blk + 1)), full,
                  pl.BlockSpec((1, 1, D), lambda b, t: (b, 0, 0)),
                  pl.BlockSpec(w.shape, lambda b, t: (0, 0), pipeline_mode=pl.Buffered(1)),
                  pl.BlockSpec((1, RET_DV), lambda b, t: (0, 0))],
        out_specs=full, scratch_shapes=[pltpu.VMEM((tm, W), BF16)],
        compiler_params=_params("parallel", "parallel"), name="outproj_odd",
    )(o_g, o_f, o_b, p, p, x, g5, w, ret_g.reshape(1, RET_DV))


def _rope_tables(n_tokens):
    rows = n_tokens // GRID_W
    row = jnp.repeat(jnp.arange(rows), GRID_W)
    col = jnp.tile(jnp.arange(GRID_W), rows)
    pos = jnp.stack([row, col], axis=-1).astype(F32)
    inv = ROPE_THETA ** (-jnp.arange(ROPE_QUARTER, dtype=F32) / ROPE_QUARTER)
    ang = pos[:, :, None] * inv
    cos, sin = jnp.cos(ang), jnp.sin(ang)
    zero = jnp.zeros_like(sin)
    lanes = lambda a, b: jnp.stack([a, b], axis=2).reshape(n_tokens, HEAD_DIM)
    return lanes(cos, cos), lanes(-sin, zero), lanes(zero, sin)


def _even_mixer(x, ctx, mx, mc, g, w_in, w_out, lb, hg_g, lam_p, da_g, layer_idx, tables, ctx_out):
    B = x.shape[0]
    lam_init = 0.8 - 0.6 * math.exp(-0.3 * layer_idx)
    lam = jnp.exp(jnp.sum(lam_p[0] * lam_p[1])) - jnp.exp(jnp.sum(lam_p[2] * lam_p[3])) + lam_init
    lam_row = jnp.full((1, DA_DV), lam, F32)
    da_gain = da_g.reshape(1, DA_DV)
    px = _inproj_call(x, mx[3], mx[4], g, w_in, tn=1024)
    pc = _inproj_call(ctx, mc[3], mc[4], g, w_in, tn=1024)

    s0 = jnp.zeros((2, B, HG_HEADS, HEAD_DIM, HEAD_DIM), F32)
    ocf, ocb, s_c = _gla_call(pc, lb, s0)
    oxf, oxb, _ = _gla_call(px, lb, s_c)

    qx, kx, vx = _prep_even_call(px, tables)
    qc, kc, vc = _prep_even_call(pc, None)
    attn = functools.partial(_attn_call, n_groups=DA_HEADS, n_q=2, shared_k=False, dv=DA_DV, diff=True,
                             lam=lam_row, gain=da_gain, lam_init=lam_init, tq=1024)
    da_x = attn(qx, kx, vx, kc, vc)
    x_new = _outproj_even_call(oxf, oxb, px, da_x, x, mx[5], w_out, hg_g)
    if not ctx_out:
        return x_new, None
    da_c = attn(qc, None, None, kc, vc)
    ctx_new = _outproj_even_call(ocf, ocb, pc, da_c, ctx, mc[5], w_out, hg_g)
    return x_new, ctx_new


def _odd_mixer(x, ctx, mx, mc, g, w_in, w_out, q_g, k_g, decay_logit, ret_g, tables, ctx_out):
    B = x.shape[0]
    px = _inproj_call(x, mx[3], mx[4], g, w_in, tn=w_in.shape[1] // 3)
    pc = _inproj_call(ctx, mc[3], mc[4], g, w_in, tn=w_in.shape[1] // 3)

    qx, kx, vx = _prep_odd_call(px, q_g, k_g, tables)
    qc, kc, vc = _prep_odd_call(pc, q_g, k_g, None)
    attn = functools.partial(_attn_call, n_groups=GQA_KV_HEADS, n_q=GQA_HEADS // GQA_KV_HEADS, shared_k=True,
                             dv=HEAD_DIM, tq=512)
    og_x = attn(qx, kx, vx, kc, vc)

    log_gamma = jax.nn.log_sigmoid(decay_logit.astype(F32))
    lg = jnp.broadcast_to(log_gamma[:, :, None, None], (2, RET_HEADS, 1, HEAD_DIM))
    s0 = jnp.zeros((2, B, RET_HEADS, HEAD_DIM, RET_DV), F32)
    ocf, ocb, s_c = _ret_call(pc, lg, s0, None)
    oxf, oxb, _ = _ret_call(px, lg, s_c, tables)
    x_new = _outproj_odd_call(og_x, oxf, oxb, px, x, mx[5], w_out, ret_g)
    if not ctx_out:
        return x_new, None
    og_c = attn(qc, None, None, kc, vc)
    ctx_new = _outproj_odd_call(og_c, ocf, ocb, pc, ctx, mc[5], w_out, ret_g)
    return x_new, ctx_new


def kernel(x, c, ctx, c_ctx, mod_w, mod_b, norm_g, ffn_w13, ffn_w2, even_w_in, even_w_out, hgrn_lb_logits, hgrn_norm_g, diff_lambda, diff_norm_g, odd_w_in, odd_w_out, gqa_q_norm_g, gqa_k_norm_g, ret_decay_logit, ret_norm_g, final_norm_g):
    B, T, D = x.shape
    C = ctx.shape[1]
    depth = mod_w.shape[0]
    tables = _rope_tables(T)
    lower_bounds = jnp.cumsum(jax.nn.softmax(hgrn_lb_logits.astype(F32), axis=0), axis=0)

    rows = -(-(B + 1) // 8) * 8
    cc = jnp.zeros((rows, D), F32).at[:B].set(c).at[B].set(c_ctx)
    mod = _mod_call(cc, mod_w, mod_b)
    mod_x = mod[:, :B].reshape(depth, B, N_MOD, D).transpose(0, 2, 1, 3)[:, :, :, None, :]
    mod_c = mod[:, B].reshape(depth, N_MOD, 1, 1, D)

    w13 = ffn_w13.astype(BF16)
    w2 = ffn_w2.astype(BF16)
    even_in, even_out = even_w_in.astype(BF16), even_w_out.astype(BF16)
    odd_in, odd_out = odd_w_in.astype(BF16), odd_w_out.astype(BF16)

    ctx = ctx.reshape(1, B * C, D)
    for l in range(depth):
        last = l == depth - 1
        mx, mc = mod_x[l], mod_c[l]
        g = norm_g[l]
        x = _ffn_call(x, mx[0], mx[1], mx[2], g[0], w13, w2, l, 0)
        ctx = _ffn_call(ctx, mc[0], mc[1], mc[2], g[0], w13, w2, l, 0)
        ctx_b = ctx.reshape(B, C, D)
        mc_b = jnp.broadcast_to(mc, (N_MOD, B, 1, D))
        if l % 2 == 0:
            e = l // 2
            x, ctx_b = _even_mixer(x, ctx_b, mx, mc_b, g[1], even_in[e], even_out[e], lower_bounds[l], hgrn_norm_g[e],
                                   diff_lambda[e], diff_norm_g[e], l, tables, not last)
        else:
            o = l // 2
            x, ctx_b = _odd_mixer(x, ctx_b, mx, mc_b, g[1], odd_in[o], odd_out[o], gqa_q_norm_g[o], gqa_k_norm_g[o],
                                  ret_decay_logit[o], ret_norm_g[o], tables, not last)
        x = _ffn_call(x, mx[6], mx[7], mx[8], g[2], w13, w2, l, 1, final_g=final_norm_g if last else None)
        if not last:
            ctx = _ffn_call(ctx_b.reshape(1, B * C, D), mc[6], mc[7], mc[8], g[2], w13, w2, l, 1)
    return x
```
